```python
import math
import jax
import jax.numpy as jnp
from jax import lax
import numpy as np

D_MODEL = 2048
BATCH = 1
SEQ = 16384
DEPTH = 2

GRID_W = 64
CTX_LEN = 256
HEAD_DIM = 128
N_HEADS = D_MODEL // HEAD_DIM
MIX_WIDTH = N_HEADS * HEAD_DIM
A_HEADS = N_HEADS // 4
C_HEADS = N_HEADS // 4
B_HEADS = N_HEADS - A_HEADS - C_HEADS
B_KV_HEADS = B_HEADS // 4
B_GROUP = B_HEADS // B_KV_HEADS
A_QK_DIM = HEAD_DIM // 2
BLOCK = 128
WINDOW = 128
NA_KH = 8
NA_KW = 16
NA_ROW_BLOCK = BLOCK // GRID_W
PEER_HEADS = 8
PEER_KEYS = 128
PEER_EXPERTS = PEER_KEYS * PEER_KEYS
PEER_QDIM = 256
PEER_HALF = PEER_QDIM // 2
PEER_TOPK = 16
ROPE_BASE = 10000.0
LN_EPS = 1e-5
DN_ALPHA = (2 * DEPTH) ** 0.25
DN_BETA = (8 * DEPTH) ** -0.25
IN_WIDTHS = (A_HEADS * 2 * A_QK_DIM, A_HEADS * 2 * A_QK_DIM, A_HEADS * HEAD_DIM,
             B_HEADS * HEAD_DIM, B_KV_HEADS * HEAD_DIM, B_KV_HEADS * HEAD_DIM,
             C_HEADS * HEAD_DIM, C_HEADS * HEAD_DIM, C_HEADS * HEAD_DIM)
IN_WIDTH = sum(IN_WIDTHS)
IN_SPLITS = tuple(int(v) for v in np.cumsum(IN_WIDTHS)[:-1])

kernel_name = 'hybrid_diffattn_swa_natten_peer_dit'

F32 = jnp.float32


def layer_norm(x, g, b):
    xf = x.astype(F32)
    mu = xf.mean(-1, keepdims=True)
    var = jnp.square(xf - mu).mean(-1, keepdims=True)
    y = (xf - mu) * lax.rsqrt(var + LN_EPS) * g.astype(F32) + b.astype(F32)
    return y.astype(x.dtype)


def rms_norm(x, w):
    xf = x.astype(F32)
    y = xf * lax.rsqrt(jnp.mean(jnp.square(xf), -1, keepdims=True) + LN_EPS) * w.astype(F32)
    return y.astype(x.dtype)


def rope_tables(n, dim, dtype):
    d_axis = dim // 2
    inv = ROPE_BASE ** (-jnp.arange(0, d_axis, 2, dtype=F32) / d_axis)
    t = jnp.arange(n)
    row = (t // GRID_W).astype(F32)
    col = (t % GRID_W).astype(F32)
    ang = jnp.concatenate([row[:, None] * inv, col[:, None] * inv], axis=-1)
    return jnp.cos(ang).astype(dtype), jnp.sin(ang).astype(dtype)


def apply_rope(x, cos, sin):
    x1 = x[..., 0::2]
    x2 = x[..., 1::2]
    return jnp.stack([x1 * cos - x2 * sin, x1 * sin + x2 * cos], axis=-1).reshape(x.shape)


def heads(t, h, d):
    b, n, _ = t.shape
    return t.reshape(b, n, h, d).transpose(0, 2, 1, 3)


def merge(t):
    b, h, n, d = t.shape
    return t.transpose(0, 2, 1, 3).reshape(b, n, h * d)


def diff_attend(q, k, v, lam):
    s = jnp.einsum('bhmqd,bhmkd->bhmqk', q, k).astype(F32) * (A_QK_DIM ** -0.5)
    p = jax.nn.softmax(s, axis=-1)
    w = p[:, :, 0] - lam * p[:, :, 1]
    return jnp.einsum('bhqk,bhkd->bhqd', w.astype(v.dtype), v)


def mixer_diff(q_x, k_x, v_x, q_h, k_h, v_h, lam, subln_w, lam_init, with_ctx):
    b, h, _, s, dk = q_x.shape
    nblk = s // BLOCK
    k_all = jnp.concatenate([k_h, k_x], axis=3)
    v_all = jnp.concatenate([v_h, v_x], axis=2)
    qb = q_x.reshape(b, h, 2, nblk, BLOCK, dk).transpose(3, 0, 1, 2, 4, 5)
    o = lax.map(lambda q: diff_attend(q, k_all, v_all, lam), qb)
    o = o.transpose(1, 2, 0, 3, 4).reshape(b, h, s, -1)
    o_x = merge(rms_norm(o, subln_w) * (1.0 - lam_init))
    o_h = None
    if with_ctx:
        o_h = merge(rms_norm(diff_attend(q_h, k_h, v_h, lam), subln_w) * (1.0 - lam_init))
    return o_x, o_h


def mixer_window(q_x, k_x, v_x, q_h, k_h, v_h, sink, with_ctx):
    b, kv, g, s, d = q_x.shape
    n_ctx = k_h.shape[2]
    nblk = s // BLOCK
    scale = d ** -0.5
    dt = v_x.dtype
    qb = q_x.reshape(b, kv, g, nblk, BLOCK, d)

    def band(t):
        tp = jnp.pad(t, ((0, 0), (0, 0), (BLOCK, BLOCK), (0, 0))).reshape(b, kv, nblk + 2, BLOCK, d)
        return jnp.concatenate([tp[:, :, :-2], tp[:, :, 1:-1], tp[:, :, 2:]], axis=3)

    kb = band(k_x)
    vb = band(v_x)
    qpos = jnp.arange(nblk)[:, None] * BLOCK + jnp.arange(BLOCK)[None, :]
    kpos = jnp.arange(nblk)[:, None] * BLOCK - BLOCK + jnp.arange(3 * BLOCK)[None, :]
    rel = kpos[:, None, :] - qpos[:, :, None]
    valid = (jnp.abs(rel) <= WINDOW) & (kpos[:, None, :] >= 0) & (kpos[:, None, :] < s)
    s_loc = jnp.einsum('bkgnqd,bkncd->bkgnqc', qb, kb).astype(F32) * scale
    s_loc = jnp.where(valid, s_loc, -jnp.inf)
    s_ctx = jnp.einsum('bkgnqd,bkcd->bkgnqc', qb, k_h).astype(F32) * scale
    snk = jnp.broadcast_to(sink.astype(F32).reshape(1, kv, g, 1, 1, 1), s_ctx.shape[:-1] + (1,))
    p = jax.nn.softmax(jnp.concatenate([snk, s_ctx, s_loc], axis=-1), axis=-1)
    o = (jnp.einsum('bkgnqc,bkcd->bkgnqd', p[..., 1:1 + n_ctx].astype(dt), v_h)
         + jnp.einsum('bkgnqc,bkncd->bkgnqd', p[..., 1 + n_ctx:].astype(dt), vb))
    o_x = o.reshape(b, kv, g, s, d).transpose(0, 3, 1, 2, 4).reshape(b, s, kv * g * d)
    o_h = None
    if with_ctx:
        sc = jnp.einsum('bkgqd,bkcd->bkgqc', q_h, k_h).astype(F32) * scale
        snk_c = jnp.broadcast_to(sink.astype(F32).reshape(1, kv, g, 1, 1), sc.shape[:-1] + (1,))
        pc = jax.nn.softmax(jnp.concatenate([snk_c, sc], axis=-1), axis=-1)[..., 1:]
        oc = jnp.einsum('bkgqc,bkcd->bkgqd', pc.astype(dt), v_h)
        o_h = oc.transpose(0, 3, 1, 2, 4).reshape(b, n_ctx, kv * g * d)
    return o_x, o_h


def na_tables(rows, rpb):
    kh = min(NA_KH, rows)
    kw = NA_KW
    r = jnp.arange(rows)
    cl = jnp.arange(GRID_W)
    key_r = jnp.clip(r - kh // 2, 0, rows - kh)[:, None] + jnp.arange(kh)[None, :]
    key_c = jnp.clip(cl - kw // 2, 0, GRID_W - kw)[:, None] + jnp.arange(kw)[None, :]
    idx = key_r[:, None, :, None] * GRID_W + key_c[None, :, None, :]
    dr = (key_r - r[:, None]) + (NA_KH - 1)
    dc = (key_c - cl[:, None]) + (NA_KW - 1)
    bias = rpb[:, dr[:, None, :, None], dc[None, :, None, :]]
    nrb = rows // NA_ROW_BLOCK
    idx = idx.reshape(nrb, BLOCK, kh * kw)
    bias = bias.reshape(rpb.shape[0], nrb, BLOCK, kh * kw).transpose(1, 0, 2, 3)
    return idx, bias


def mixer_na(q_x, k_x, v_x, q_h, k_h, v_h, rpb, with_ctx):
    b, h, s, d = q_x.shape
    n_ctx = k_h.shape[2]
    rows = s // GRID_W
    scale = d ** -0.5
    dt = v_x.dtype
    idx, bias = na_tables(rows, rpb)
    nrb = idx.shape[0]
    qb = q_x.reshape(b, h, nrb, BLOCK, d).transpose(2, 0, 1, 3, 4)

    def attend(args):
        q, ib, bb = args
        kg = jnp.take(k_x, ib, axis=2)
        vg = jnp.take(v_x, ib, axis=2)
        s_nb = jnp.einsum('bhqd,bhqkd->bhqk', q, kg).astype(F32) * scale + bb[None].astype(F32)
        s_ctx = jnp.einsum('bhqd,bhcd->bhqc', q, k_h).astype(F32) * scale
        p = jax.nn.softmax(jnp.concatenate([s_ctx, s_nb], axis=-1), axis=-1)
        return (jnp.einsum('bhqc,bhcd->bhqd', p[..., :n_ctx].astype(dt), v_h)
                + jnp.einsum('bhqk,bhqkd->bhqd', p[..., n_ctx:].astype(dt), vg))

    o = lax.map(attend, (qb, idx, bias))
    o_x = merge(o.transpose(1, 2, 0, 3, 4).reshape(b, h, s, d))
    o_h = None
    if with_ctx:
        sc = jnp.einsum('bhqd,bhcd->bhqc', q_h, k_h).astype(F32) * scale
        pc = jax.nn.softmax(sc, axis=-1)
        o_h = merge(jnp.einsum('bhqc,bhcd->bhqd', pc.astype(dt), v_h))
    return o_x, o_h


def token_mixers(ux, uh, w_in_l, lam, subln_w, lam_init, sink_l, rpb_l, cos_a, sin_a, cos_b, sin_b, with_ctx):
    px = jnp.split(ux @ w_in_l, IN_SPLITS, axis=-1)
    ph = jnp.split(uh @ w_in_l, IN_SPLITS, axis=-1)

    def a_qk(t):
        b, n, _ = t.shape
        return t.reshape(b, n, A_HEADS, 2, A_QK_DIM).transpose(0, 2, 3, 1, 4)

    def b_q(t):
        b, n, _ = t.shape
        return t.reshape(b, n, B_KV_HEADS, B_GROUP, HEAD_DIM).transpose(0, 2, 3, 1, 4)

    oa_x, oa_h = mixer_diff(
        apply_rope(a_qk(px[0]), cos_a, sin_a), apply_rope(a_qk(px[1]), cos_a, sin_a), heads(px[2], A_HEADS, HEAD_DIM),
        a_qk(ph[0]), a_qk(ph[1]), heads(ph[2], A_HEADS, HEAD_DIM), lam, subln_w, lam_init, with_ctx)
    ob_x, ob_h = mixer_window(
        apply_rope(b_q(px[3]), cos_b, sin_b), apply_rope(heads(px[4], B_KV_HEADS, HEAD_DIM), cos_b, sin_b),
        heads(px[5], B_KV_HEADS, HEAD_DIM),
        b_q(ph[3]), heads(ph[4], B_KV_HEADS, HEAD_DIM), heads(ph[5], B_KV_HEADS, HEAD_DIM), sink_l, with_ctx)
    oc_x, oc_h = mixer_na(
        heads(px[6], C_HEADS, HEAD_DIM), heads(px[7], C_HEADS, HEAD_DIM), heads(px[8], C_HEADS, HEAD_DIM),
        heads(ph[6], C_HEADS, HEAD_DIM), heads(ph[7], C_HEADS, HEAD_DIM), heads(ph[8], C_HEADS, HEAD_DIM),
        rpb_l, with_ctx)
    m_x = jnp.concatenate([oa_x, ob_x, oc_x], axis=-1)
    m_h = jnp.concatenate([oa_h, ob_h, oc_h], axis=-1) if with_ctx else None
    return m_x, m_h


def peer(u, wq, sub_keys, u_tab, v_tab):
    b, t, d = u.shape
    q = (u @ wq).reshape(b, t, PEER_HEADS, 2, PEER_HALF)
    s = jnp.einsum('bthpd,hpnd->bthpn', q, sub_keys).astype(F32)
    v1, i1 = lax.top_k(s[..., 0, :], PEER_TOPK)
    v2, i2 = lax.top_k(s[..., 1, :], PEER_TOPK)
    cand = (v1[..., :, None] + v2[..., None, :]).reshape(b, t, PEER_HEADS, PEER_TOPK * PEER_TOPK)
    vs, ic = lax.top_k(cand, PEER_TOPK)
    e1 = jnp.take_along_axis(i1, ic // PEER_TOPK, axis=-1)
    e2 = jnp.take_along_axis(i2, ic % PEER_TOPK, axis=-1)
    experts = e1 * PEER_KEYS + e2
    gates = jax.nn.softmax(vs, axis=-1)
    nb = (b * t) // BLOCK
    ub = u.reshape(nb, BLOCK, d)
    eb = experts.reshape(nb, BLOCK, PEER_HEADS, PEER_TOPK)
    gb = gates.reshape(nb, BLOCK, PEER_HEADS, PEER_TOPK)

    def expert_block(args):
        xb, e, g = args
        a = jax.nn.gelu(jnp.einsum('td,thkd->thk', xb, u_tab[e]).astype(F32), approximate=False)
        return jnp.einsum('thk,thkd->td', (g * a).astype(xb.dtype), v_tab[e])

    y = lax.map(expert_block, (ub, eb, gb))
    return y.reshape(b, t, d)


def setup_inputs(seed: int = 0) -> dict:
    key = jax.random.key(seed)
    ks = jax.random.split(key, 24)

    def nrm(k, shape, std):
        return jax.random.normal(k, shape, F32) * std

    return {
        'x': nrm(ks[0], (BATCH, SEQ, D_MODEL), 1.0),
        'c': nrm(ks[1], (BATCH, D_MODEL), 1.0),
        'ctx': nrm(ks[2], (BATCH, CTX_LEN, D_MODEL), 1.0),
        'c_ctx': nrm(ks[3], (D_MODEL,), 1.0),
        'w_ada': nrm(ks[4], (DEPTH, D_MODEL, 6 * D_MODEL), 0.3 * D_MODEL ** -0.5),
        'b_ada': nrm(ks[5], (DEPTH, 6 * D_MODEL), 0.02),
        'w_in': nrm(ks[6], (DEPTH, D_MODEL, IN_WIDTH), D_MODEL ** -0.5),
        'w_out': nrm(ks[7], (DEPTH, MIX_WIDTH, D_MODEL), DN_BETA * MIX_WIDTH ** -0.5),
        'beta_out': 1.0 + nrm(ks[8], (DEPTH, MIX_WIDTH), 0.02),
        'ln1_g': 1.0 + nrm(ks[9], (DEPTH, D_MODEL), 0.02),
        'ln1_b': nrm(ks[10], (DEPTH, D_MODEL), 0.02),
        'ln2_g': 1.0 + nrm(ks[11], (DEPTH, D_MODEL), 0.02),
        'ln2_b': nrm(ks[12], (DEPTH, D_MODEL), 0.02),
        'diff_lq1': nrm(ks[13], (DEPTH, A_QK_DIM), 0.1),
        'diff_lk1': nrm(ks[14], (DEPTH, A_QK_DIM), 0.1),
        'diff_lq2': nrm(ks[15], (DEPTH, A_QK_DIM), 0.1),
        'diff_lk2': nrm(ks[16], (DEPTH, A_QK_DIM), 0.1),
        'diff_subln': 1.0 + nrm(ks[17], (DEPTH, HEAD_DIM), 0.02),
        'sink': nrm(ks[18], (DEPTH, B_HEADS), 0.5),
        'na_rpb': nrm(ks[19], (DEPTH, C_HEADS, 2 * NA_KH - 1, 2 * NA_KW - 1), 0.1),
        'peer_wq': nrm(ks[20], (DEPTH, D_MODEL, PEER_HEADS * PEER_QDIM), D_MODEL ** -0.5),
        'peer_keys': nrm(ks[21], (DEPTH, PEER_HEADS, 2, PEER_KEYS, PEER_HALF), PEER_HALF ** -0.5),
        'peer_u': nrm(ks[22], (DEPTH, PEER_EXPERTS, D_MODEL), D_MODEL ** -0.5),
        'peer_v': nrm(ks[23], (DEPTH, PEER_EXPERTS, D_MODEL), DN_BETA * PEER_HEADS ** -0.5),
    }


def reference(x, c, ctx, c_ctx, w_ada, b_ada, w_in, w_out, beta_out, ln1_g, ln1_b, ln2_g, ln2_b,
              diff_lq1, diff_lk1, diff_lq2, diff_lk2, diff_subln, sink, na_rpb,
              peer_wq, peer_keys, peer_u, peer_v):
    s = x.shape[1]
    cos_a, sin_a = rope_tables(s, A_QK_DIM, x.dtype)
    cos_b, sin_b = rope_tables(s, HEAD_DIM, x.dtype)
    h = ctx
    for l in range(DEPTH):
        with_ctx = l < DEPTH - 1
        lam_init = 0.8 - 0.6 * math.exp(-0.3 * l)
        lam = (jnp.exp(jnp.sum(diff_lq1[l].astype(F32) * diff_lk1[l].astype(F32)))
               - jnp.exp(jnp.sum(diff_lq2[l].astype(F32) * diff_lk2[l].astype(F32))) + lam_init)
        mod_x = jax.nn.silu(c) @ w_ada[l] + b_ada[l]
        mod_h = jax.nn.silu(c_ctx) @ w_ada[l] + b_ada[l]
        sh1, sc1, g1, sh2, sc2, g2 = jnp.split(mod_x[:, None, :], 6, axis=-1)
        ch1, cc1, cg1, ch2, cc2, cg2 = jnp.split(mod_h, 6, axis=-1)

        ux = x * (1.0 + sc1) + sh1
        uh = h * (1.0 + cc1) + ch1
        m_x, m_h = token_mixers(ux, uh, w_in[l], lam, diff_subln[l], lam_init, sink[l], na_rpb[l],
                                cos_a, sin_a, cos_b, sin_b, with_ctx)
        x = layer_norm(DN_ALPHA * x + g1 * ((m_x * beta_out[l]) @ w_out[l]), ln1_g[l], ln1_b[l])
        ux = x * (1.0 + sc2) + sh2
        x = layer_norm(DN_ALPHA * x + g2 * peer(ux, peer_wq[l], peer_keys[l], peer_u[l], peer_v[l]),
                       ln2_g[l], ln2_b[l])
        if with_ctx:
            h = layer_norm(DN_ALPHA * h + cg1 * ((m_h * beta_out[l]) @ w_out[l]), ln1_g[l], ln1_b[l])
            uh = h * (1.0 + cc2) + ch2
            h = layer_norm(DN_ALPHA * h + cg2 * peer(uh, peer_wq[l], peer_keys[l], peer_u[l], peer_v[l]),
                           ln2_g[l], ln2_b[l])
    return x
```

```python
import functools
import math

import numpy as np
import jax
import jax.numpy as jnp
from jax import lax
from jax.experimental import pallas as pl
from jax.experimental.pallas import tpu as pltpu

F32 = jnp.float32
BF16 = jnp.bfloat16

GRID_W = 64
HEAD_DIM = 128
A_HEADS = 4
B_HEADS = 8
B_KV_HEADS = 2
B_GROUP = 4
C_HEADS = 4
A_QK_DIM = 64
WINDOW = 128
NA_KH = 8
NA_KW = 16
PEER_HEADS = 8
PEER_KEYS = 128
PEER_TOPK = 16
ROPE_BASE = 10000.0
LN_EPS = 1e-5

LANES = 128
TOKEN_TILE = 512
VMEM_LIMIT = 56 * 1024 * 1024

COL_AQ, COL_AK, COL_AV = 0, 4, 8
COL_BQ, COL_BK, COL_BV = 12, 20, 22
COL_CQ, COL_CK, COL_CV = 24, 28, 32
N_COLBLOCKS = 36


def _cparams(sem):
    return pltpu.CompilerParams(dimension_semantics=sem, vmem_limit_bytes=VMEM_LIMIT)


def _dot_nt(a, b):
    return lax.dot_general(a, b, (((1,), (1,)), ((), ())), preferred_element_type=F32)


def _dot_tn(a, b):
    return lax.dot_general(a, b, (((0,), (0,)), ((), ())), preferred_element_type=F32)


def _layer_norm(y, g, b):
    mu = jnp.mean(y, axis=-1, keepdims=True)
    d = y - mu
    var = jnp.mean(d * d, axis=-1, keepdims=True)
    return d * lax.rsqrt(var + LN_EPS) * g + b


def _ada_kernel(c_ref, w_ref, b_ref, o_ref):
    c = c_ref[...]
    a = c * jax.nn.sigmoid(c)
    o_ref[0] = jnp.dot(a.astype(BF16), w_ref[0].astype(BF16), preferred_element_type=F32) + b_ref[0]


def _ada_modulation(cc, w_ada, b_ada):
    depth, d, n = w_ada.shape
    tn = 1024
    return pl.pallas_call(
        _ada_kernel,
        grid=(depth, n // tn),
        in_specs=[
            pl.BlockSpec((8, d), lambda l, j: (0, 0)),
            pl.BlockSpec((1, d, tn), lambda l, j: (l, 0, j)),
            pl.BlockSpec((1, 1, tn), lambda l, j: (l, 0, j)),
        ],
        out_specs=pl.BlockSpec((1, 8, tn), lambda l, j: (l, 0, j)),
        out_shape=jax.ShapeDtypeStruct((depth, 8, n), F32),
        compiler_params=_cparams(("arbitrary", "arbitrary")),
        name="ada_modulation",
    )(cc, w_ada, b_ada.reshape(depth, 1, n))


def _rope_kind(colblock):
    if colblock < COL_AV:
        return 0
    if COL_BQ <= colblock < COL_BV:
        return 1
    return None


def _proj_kernel(x_ref, sc_ref, sh_ref, w_ref, *rest, tn, rope):
    if rope:
        cos_ref, sin_ref, o_ref = rest
    else:
        (o_ref,) = rest
    xm = (x_ref[...] * (1.0 + sc_ref[0]) + sh_ref[0]).astype(BF16)
    n = w_ref.shape[1]
    for jn in range(n // tn):
        y = jnp.dot(xm, w_ref[:, jn * tn:(jn + 1) * tn], preferred_element_type=F32)
        for jb in range(tn // LANES):
            colblock = jn * (tn // LANES) + jb
            yb = y[:, jb * LANES:(jb + 1) * LANES]
            kind = _rope_kind(colblock) if rope else None
            if kind is not None:
                yb = yb * cos_ref[kind] + pltpu.roll(yb, LANES // 2, 1) * sin_ref[kind]
            o_ref[:, colblock * LANES:(colblock + 1) * LANES] = yb.astype(o_ref.dtype)


def _mod_index(layer_slot, chunk, n_lat_tiles):
    def index(i):
        who = jnp.where(i >= n_lat_tiles, 1, 0)
        return ((layer_slot * 2 + who) * 6 + chunk, 0, 0)
    return index


def _projection(xs, modv, layer, chunk_scale, chunk_shift, w, n_lat_tiles, rope_tabs=None):
    tp, d = xs.shape
    n = w.shape[1]
    tm = TOKEN_TILE
    rope = rope_tabs is not None
    in_specs = [
        pl.BlockSpec((tm, d), lambda i: (i, 0)),
        pl.BlockSpec((1, 1, d), _mod_index(layer, chunk_scale, n_lat_tiles)),
        pl.BlockSpec((1, 1, d), _mod_index(layer, chunk_shift, n_lat_tiles)),
        pl.BlockSpec((d, n), lambda i: (0, 0), pipeline_mode=pl.Buffered(1)),
    ]
    args = [xs, modv, modv, w]
    if rope:
        in_specs += [pl.BlockSpec((2, tm, LANES), lambda i: (0, i, 0))] * 2
        args += list(rope_tabs)
    return pl.pallas_call(
        functools.partial(_proj_kernel, tn=512, rope=rope),
        grid=(tp // tm,),
        in_specs=in_specs,
        out_specs=pl.BlockSpec((tm, n), lambda i: (i, 0)),
        out_shape=jax.ShapeDtypeStruct((tp, n), BF16),
        compiler_params=_cparams(("arbitrary",)),
        name="mod_projection_rope" if rope else "mod_projection",
    )(*args)


def _attn_a_kernel(lq1_ref, lk1_ref, lq2_ref, lk2_ref, subln_ref, beta_ref, q_ref, k_ref, v_ref, o_ref,
                   qs_sc, m_sc, l_sc, acc_sc, *, s_lat, n_ctx, tq, tc, n_lat_tiles, lam_init):
    i = pl.program_id(1)
    lane = lax.broadcasted_iota(jnp.int32, (1, LANES), 1)
    map1 = ((lane // 32) % 2) == 1
    q = q_ref[...] * (A_QK_DIM ** -0.5)
    zero = jnp.zeros_like(q)
    qs_sc[0:tq, :] = jnp.where(map1, zero, q)
    qs_sc[tq:2 * tq, :] = jnp.where(map1, q, zero)
    m_sc[...] = jnp.full_like(m_sc, -jnp.inf)
    l_sc[...] = jnp.zeros_like(l_sc)
    acc_sc[...] = jnp.zeros_like(acc_sc)

    def attend(kc, vc):
        s = _dot_nt(qs_sc[...], kc)
        m_prev = m_sc[...]
        m_new = jnp.maximum(m_prev, jnp.max(s, axis=-1, keepdims=True))
        p = jnp.exp(s - m_new)
        alpha = jnp.exp(m_prev - m_new)
        l_sc[...] = alpha * l_sc[...] + jnp.sum(p, axis=-1, keepdims=True)
        acc_sc[...] = alpha * acc_sc[...] + jnp.dot(p.astype(BF16), vc, preferred_element_type=F32)
        m_sc[...] = m_new

    @pl.when(i < n_lat_tiles)
    def _():
        def body(c, carry):
            start = pl.multiple_of(c * tc, tc)
            attend(k_ref[pl.ds(start, tc), :], v_ref[pl.ds(start, tc), :])
            return carry
        lax.fori_loop(0, s_lat // tc, body, 0)

    attend(k_ref[s_lat:s_lat + n_ctx, :], v_ref[s_lat:s_lat + n_ctx, :])

    lam = (jnp.exp(jnp.sum(lq1_ref[...] * lk1_ref[...], axis=-1, keepdims=True))
           - jnp.exp(jnp.sum(lq2_ref[...] * lk2_ref[...], axis=-1, keepdims=True)) + lam_init)
    o = acc_sc[0:tq, :] / l_sc[0:tq, :] - lam * (acc_sc[tq:2 * tq, :] / l_sc[tq:2 * tq, :])
    y = o * lax.rsqrt(jnp.mean(o * o, axis=-1, keepdims=True) + LN_EPS) * subln_ref[...]
    o_ref[...] = (y * (1.0 - lam_init) * beta_ref[...]).astype(o_ref.dtype)


def _attention_a(proj, lq1, lk1, lq2, lk2, subln, beta, s_lat, n_ctx, lam_init):
    tp = proj.shape[0]
    tq, tc = 256, 512
    n_lat_tiles = s_lat // tq
    vec = lambda: pl.BlockSpec((1, A_QK_DIM), lambda h, i: (0, 0))
    return pl.pallas_call(
        functools.partial(_attn_a_kernel, s_lat=s_lat, n_ctx=n_ctx, tq=tq, tc=tc, n_lat_tiles=n_lat_tiles,
                          lam_init=lam_init),
        grid=(A_HEADS, tp // tq),
        in_specs=[
            vec(), vec(), vec(), vec(),
            pl.BlockSpec((1, HEAD_DIM), lambda h, i: (0, 0)),
            pl.BlockSpec((1, HEAD_DIM), lambda h, i: (0, h)),
            pl.BlockSpec((tq, HEAD_DIM), lambda h, i: (i, COL_AQ + h)),
            pl.BlockSpec((tp, HEAD_DIM), lambda h, i: (0, COL_AK + h)),
            pl.BlockSpec((tp, HEAD_DIM), lambda h, i: (0, COL_AV + h)),
        ],
        out_specs=pl.BlockSpec((tq, HEAD_DIM), lambda h, i: (i, h)),
        out_shape=jax.ShapeDtypeStruct((tp, A_HEADS * HEAD_DIM), BF16),
        scratch_shapes=[
            pltpu.VMEM((2 * tq, HEAD_DIM), BF16),
            pltpu.VMEM((2 * tq, 1), F32),
            pltpu.VMEM((2 * tq, 1), F32),
            pltpu.VMEM((2 * tq, HEAD_DIM), F32),
        ],
        compiler_params=_cparams(("arbitrary", "arbitrary")),
        name="attn_diff",
    )(lq1, lk1, lq2, lk2, subln, beta, proj, proj, proj)


def _attn_b_kernel(sink_ref, beta_ref, q_ref, kp_ref, kc_ref, kn_ref, kx_ref, vp_ref, vc_ref, vn_ref, vx_ref,
                   o_ref, *, s_lat, tq):
    kv = pl.program_id(0)
    i = pl.program_id(1)
    scale = HEAD_DIM ** -0.5
    q0 = i * tq
    qpos = q0 + lax.broadcasted_iota(jnp.int32, (tq, 1), 0)

    def valid(kstart, n):
        kpos = kstart + lax.broadcasted_iota(jnp.int32, (1, n), 1)
        return (jnp.abs(kpos - qpos) <= WINDOW) & (kpos >= 0) & (kpos < s_lat) & (qpos < s_lat)

    ok_p = valid(q0 - WINDOW, WINDOW)
    ok_c = valid(q0, tq)
    ok_n = valid(q0 + tq, WINDOW)
    neg = -jnp.inf
    for g in range(B_GROUP):
        qg = q_ref[:, g * HEAD_DIM:(g + 1) * HEAD_DIM]
        s_x = _dot_nt(qg, kx_ref[...]) * scale
        s_p = jnp.where(ok_p, _dot_nt(qg, kp_ref[...]) * scale, neg)
        s_c = jnp.where(ok_c, _dot_nt(qg, kc_ref[...]) * scale, neg)
        s_n = jnp.where(ok_n, _dot_nt(qg, kn_ref[...]) * scale, neg)
        snk = sink_ref[kv * B_GROUP + g]
        mx = jnp.maximum(jnp.maximum(jnp.max(s_x, -1, keepdims=True), jnp.max(s_p, -1, keepdims=True)),
                         jnp.maximum(jnp.max(s_c, -1, keepdims=True), jnp.max(s_n, -1, keepdims=True)))
        mx = jnp.maximum(mx, snk)
        e_x = jnp.exp(s_x - mx)
        e_p = jnp.exp(s_p - mx)
        e_c = jnp.exp(s_c - mx)
        e_n = jnp.exp(s_n - mx)
        den = (jnp.exp(snk - mx) + jnp.sum(e_x, -1, keepdims=True) + jnp.sum(e_p, -1, keepdims=True)
               + jnp.sum(e_c, -1, keepdims=True) + jnp.sum(e_n, -1, keepdims=True))
        o = (jnp.dot(e_x.astype(BF16), vx_ref[...], preferred_element_type=F32)
             + jnp.dot(e_p.astype(BF16), vp_ref[...], preferred_element_type=F32)
             + jnp.dot(e_c.astype(BF16), vc_ref[...], preferred_element_type=F32)
             + jnp.dot(e_n.astype(BF16), vn_ref[...], preferred_element_type=F32))
        o_ref[:, g * HEAD_DIM:(g + 1) * HEAD_DIM] = (
            (o / den) * beta_ref[:, g * HEAD_DIM:(g + 1) * HEAD_DIM]).astype(o_ref.dtype)


def _attention_b(proj, sink, beta, s_lat, n_ctx):
    tp = proj.shape[0]
    tq = 256
    per = tq // WINDOW
    last_halo = tp // WINDOW - 1
    gw = B_GROUP * HEAD_DIM

    def halo_prev(col):
        return pl.BlockSpec((WINDOW, HEAD_DIM), lambda kv, i: (jnp.maximum(i * per - 1, 0), col + kv))

    def halo_next(col):
        return pl.BlockSpec((WINDOW, HEAD_DIM), lambda kv, i: (jnp.minimum((i + 1) * per, last_halo), col + kv))

    def cur(col):
        return pl.BlockSpec((tq, HEAD_DIM), lambda kv, i: (i, col + kv))

    def ctx(col):
        return pl.BlockSpec((n_ctx, HEAD_DIM), lambda kv, i: (s_lat // n_ctx, col + kv))

    return pl.pallas_call(
        functools.partial(_attn_b_kernel, s_lat=s_lat, tq=tq),
        grid=(B_KV_HEADS, tp // tq),
        in_specs=[
            pl.BlockSpec(memory_space=pltpu.SMEM),
            pl.BlockSpec((1, gw), lambda kv, i: (0, (A_HEADS * HEAD_DIM) // gw + kv)),
            pl.BlockSpec((tq, gw), lambda kv, i: (i, (COL_BQ * LANES) // gw + kv)),
            halo_prev(COL_BK), cur(COL_BK), halo_next(COL_BK), ctx(COL_BK),
            halo_prev(COL_BV), cur(COL_BV), halo_next(COL_BV), ctx(COL_BV),
        ],
        out_specs=pl.BlockSpec((tq, gw), lambda kv, i: (i, kv)),
        out_shape=jax.ShapeDtypeStruct((tp, B_HEADS * HEAD_DIM), BF16),
        compiler_params=_cparams(("arbitrary", "arbitrary")),
        name="attn_window",
    )(sink, beta, proj, proj, proj, proj, proj, proj, proj, proj, proj)


def _na_bias_tables(rpb, s_lat):
    rows = s_lat // GRID_W
    tr = TOKEN_TILE // GRID_W
    kh = min(NA_KH, rows)

    def table(q_row0, n_rows):
        r = q_row0 + np.arange(tr)
        kr = q_row0 - tr + np.arange(3 * tr)
        c = np.arange(GRID_W)
        r0 = np.clip(r - kh // 2, 0, n_rows - kh)
        c0 = np.clip(c - NA_KW // 2, 0, GRID_W - NA_KW)
        ok_r = (kr[None, :] >= r0[:, None]) & (kr[None, :] < r0[:, None] + kh) & (kr[None, :] >= 0) & (kr[None, :] < n_rows)
        ok_c = (c[None, :] >= c0[:, None]) & (c[None, :] < c0[:, None] + NA_KW)
        dr = np.clip(kr[None, :] - r[:, None] + (NA_KH - 1), 0, 2 * NA_KH - 2)
        dc = np.clip(c[None, :] - c[:, None] + (NA_KW - 1), 0, 2 * NA_KW - 2)
        ok = ok_r[:, None, :, None] & ok_c[None, :, None, :]
        shape = ok.shape
        drb = np.broadcast_to(dr[:, None, :, None], shape).reshape(TOKEN_TILE, 3 * TOKEN_TILE)
        dcb = np.broadcast_to(dc[None, :, None, :], shape).reshape(TOKEN_TILE, 3 * TOKEN_TILE)
        return ok.reshape(TOKEN_TILE, 3 * TOKEN_TILE), drb, dcb

    far = 4 * tr
    specs = [table(0, rows), table(far, 2 * far + tr), table(rows - tr, rows)]
    out = []
    for ok, drb, dcb in specs:
        bias = rpb[:, drb, dcb]
        out.append(jnp.where(ok[None], bias, -1e30))
    out.append(jnp.full_like(out[0], -1e30))
    return jnp.stack(out, axis=0).astype(F32)


def _attn_c_kernel(beta_ref, bias_ref, q_ref, kp_ref, kc_ref, kn_ref, kx_ref, vp_ref, vc_ref, vn_ref, vx_ref,
                   o_ref, *, tq):
    scale = HEAD_DIM ** -0.5
    q = q_ref[...]
    s_x = _dot_nt(q, kx_ref[...]) * scale
    s_p = _dot_nt(q, kp_ref[...]) * scale + bias_ref[0, 0, :, 0:tq]
    s_c = _dot_nt(q, kc_ref[...]) * scale + bias_ref[0, 0, :, tq:2 * tq]
    s_n = _dot_nt(q, kn_ref[...]) * scale + bias_ref[0, 0, :, 2 * tq:3 * tq]
    mx = jnp.maximum(jnp.maximum(jnp.max(s_x, -1, keepdims=True), jnp.max(s_p, -1, keepdims=True)),
                     jnp.maximum(jnp.max(s_c, -1, keepdims=True), jnp.max(s_n, -1, keepdims=True)))
    e_x = jnp.exp(s_x - mx)
    e_p = jnp.exp(s_p - mx)
    e_c = jnp.exp(s_c - mx)
    e_n = jnp.exp(s_n - mx)
    den = (jnp.sum(e_x, -1, keepdims=True) + jnp.sum(e_p, -1, keepdims=True)
           + jnp.sum(e_c, -1, keepdims=True) + jnp.sum(e_n, -1, keepdims=True))
    o = (jnp.dot(e_x.astype(BF16), vx_ref[...], preferred_element_type=F32)
         + jnp.dot(e_p.astype(BF16), vp_ref[...], preferred_element_type=F32)
         + jnp.dot(e_c.astype(BF16), vc_ref[...], preferred_element_type=F32)
         + jnp.dot(e_n.astype(BF16), vn_ref[...], preferred_element_type=F32))
    o_ref[...] = ((o / den) * beta_ref[...]).astype(o_ref.dtype)


def _attention_c(proj, bias_tabs, beta, s_lat, n_ctx):
    tp = proj.shape[0]
    tq = TOKEN_TILE
    n_lat_tiles = s_lat // tq
    n_tiles = tp // tq
    beta_col0 = (A_HEADS + B_HEADS)

    def kind(i):
        return jnp.where(i >= n_lat_tiles, 3, jnp.where(i == 0, 0, jnp.where(i == n_lat_tiles - 1, 2, 1)))

    def prev(col):
        return pl.BlockSpec((tq, HEAD_DIM), lambda h, i: (jnp.maximum(i - 1, 0), col + h))

    def cur(col):
        return pl.BlockSpec((tq, HEAD_DIM), lambda h, i: (i, col + h))

    def nxt(col):
        return pl.BlockSpec((tq, HEAD_DIM), lambda h, i: (jnp.minimum(i + 1, n_tiles - 1), col + h))

    def ctx(col):
        return pl.BlockSpec((n_ctx, HEAD_DIM), lambda h, i: (s_lat // n_ctx, col + h))

    return pl.pallas_call(
        functools.partial(_attn_c_kernel, tq=tq),
        grid=(C_HEADS, n_tiles),
        in_specs=[
            pl.BlockSpec((1, HEAD_DIM), lambda h, i: (0, beta_col0 + h)),
            pl.BlockSpec((1, 1, tq, 3 * tq), lambda h, i: (kind(i), h, 0, 0)),
            cur(COL_CQ),
            prev(COL_CK), cur(COL_CK), nxt(COL_CK), ctx(COL_CK),
            prev(COL_CV), cur(COL_CV), nxt(COL_CV), ctx(COL_CV),
        ],
        out_specs=pl.BlockSpec((tq, HEAD_DIM), lambda h, i: (i, h)),
        out_shape=jax.ShapeDtypeStruct((tp, C_HEADS * HEAD_DIM), BF16),
        compiler_params=_cparams(("arbitrary", "arbitrary")),
        name="attn_neighbourhood",
    )(beta, bias_tabs, proj, proj, proj, proj, proj, proj, proj, proj, proj)


def _outproj_kernel(ma_ref, mb_ref, mc_ref, w_ref, x_ref, gate_ref, g_ref, b_ref, o_ref, *, alpha):
    wa = A_HEADS * HEAD_DIM
    wb = wa + B_HEADS * HEAD_DIM
    y = (jnp.dot(ma_ref[...], w_ref[0:wa, :], preferred_element_type=F32)
         + jnp.dot(mb_ref[...], w_ref[wa:wb, :], preferred_element_type=F32)
         + jnp.dot(mc_ref[...], w_ref[wb:, :], preferred_element_type=F32))
    z = alpha * x_ref[...] + gate_ref[0] * y
    o_ref[...] = _layer_norm(z, g_ref[...], b_ref[...])


def _out_projection(ma, mb, mc, w_out, xs, modv, layer, n_lat_tiles, ln_g, ln_b, alpha):
    tp, d = xs.shape
    tm = TOKEN_TILE // 2
    n_lat = n_lat_tiles * (TOKEN_TILE // tm)
    row = lambda i: (i, 0)
    fixed = lambda i: (0, 0)
    return pl.pallas_call(
        functools.partial(_outproj_kernel, alpha=alpha),
        grid=(tp // tm,),
        in_specs=[
            pl.BlockSpec((tm, ma.shape[1]), row),
            pl.BlockSpec((tm, mb.shape[1]), row),
            pl.BlockSpec((tm, mc.shape[1]), row),
            pl.BlockSpec(w_out.shape, fixed, pipeline_mode=pl.Buffered(1)),
            pl.BlockSpec((tm, d), row),
            pl.BlockSpec((1, 1, d), _mod_index(layer, 2, n_lat)),
            pl.BlockSpec((1, d), fixed),
            pl.BlockSpec((1, d), fixed),
        ],
        out_specs=pl.BlockSpec((tm, d), row),
        out_shape=jax.ShapeDtypeStruct((tp, d), F32),
        compiler_params=_cparams(("arbitrary",)),
        name="out_projection_ln",
    )(ma, mb, mc, w_out, xs, modv, ln_g, ln_b)


def _take_top(vals, rounds):
    tops, cnts = [], []
    for _ in range(rounds):
        m = jnp.max(vals, axis=0, keepdims=True)
        eq = vals == m
        tops.append(m)
        cnts.append(jnp.sum(jnp.where(eq, 1.0, 0.0), axis=0, keepdims=True))
        vals = jnp.where(eq, -jnp.inf, vals)
    return jnp.concatenate(tops, axis=0), jnp.concatenate(cnts, axis=0)


def _route_kernel(q_ref, keys_ref, s1_ref, s2_ref, p1_ref, p2_ref, thr_ref):
    k = PEER_TOPK
    for h in range(PEER_HEADS):
        s1 = _dot_nt(keys_ref[h, 0], q_ref[:, (2 * h) * LANES:(2 * h + 1) * LANES])
        s2 = _dot_nt(keys_ref[h, 1], q_ref[:, (2 * h + 1) * LANES:(2 * h + 2) * LANES])
        v1, n1 = _take_top(s1, k)
        v2, n2 = _take_top(s2, k)
        t = s1.shape[1]
        cand = jnp.concatenate([v1[r:r + 1] + v2 for r in range(k)], axis=0)
        mult = jnp.concatenate([n1[r:r + 1] * n2 for r in range(k)], axis=0)
        thr = jnp.full((1, t), jnp.inf, F32)
        seen = jnp.zeros((1, t), F32)
        rem = cand
        for _ in range(k):
            m = jnp.max(rem, axis=0, keepdims=True)
            eq = rem == m
            thr = jnp.where(seen < k, m, thr)
            seen = seen + jnp.sum(jnp.where(eq, mult, 0.0), axis=0, keepdims=True)
            rem = jnp.where(eq, -jnp.inf, rem)
        e1 = jnp.exp(v1 - v1[0:1])
        e2 = jnp.exp(v2 - v2[0:1])
        pair = jnp.concatenate([e1[r:r + 1] * e2 for r in range(k)], axis=0)
        z = jnp.sum(jnp.where(cand >= thr, mult * pair, 0.0), axis=0, keepdims=True)
        s1_ref[h] = s1
        s2_ref[h] = s2
        p1_ref[h] = jnp.exp(s1 - v1[0:1])
        p2_ref[h] = jnp.exp(s2 - v2[0:1]) / z
        thr_ref[pl.ds(h, 1), :] = thr


def _peer_route(qp, keys):
    tp = qp.shape[0]
    tt = TOKEN_TILE
    big = pl.BlockSpec((PEER_HEADS, PEER_KEYS, tt), lambda j: (0, 0, j))
    shape = jax.ShapeDtypeStruct((PEER_HEADS, PEER_KEYS, tp), F32)
    return pl.pallas_call(
        _route_kernel,
        grid=(tp // tt,),
        in_specs=[
            pl.BlockSpec((tt, qp.shape[1]), lambda j: (j, 0)),
            pl.BlockSpec(keys.shape, lambda j: (0, 0, 0, 0)),
        ],
        out_specs=[big, big, big, big, pl.BlockSpec((PEER_HEADS, tt), lambda j: (0, j))],
        out_shape=[shape, shape, shape, shape, jax.ShapeDtypeStruct((PEER_HEADS, tp), F32)],
        compiler_params=_cparams(("arbitrary",)),
        name="peer_route",
    )(qp, keys)


def _gelu(a):
    return 0.5 * a * (1.0 + lax.erf(a * (2.0 ** -0.5)))


def _peer_expert_kernel(x_ref, sc_ref, sh_ref, gate_ref, g_ref, b_ref, u_ref, v_ref,
                        s1_ref, s2_ref, p1_ref, p2_ref, thr_ref, o_ref, ux_sc, wt_sc, acc_sc,
                        *, alpha, te, tt):
    i = pl.program_id(1)

    @pl.when(i == 0)
    def _():
        ux_sc[...] = (x_ref[...] * (1.0 + sc_ref[0]) + sh_ref[0]).astype(BF16)
        acc_sc[...] = jnp.zeros_like(acc_sc)

    a_t = _dot_nt(u_ref[...], ux_sc[...])
    for a in range(te // PEER_KEYS):
        for lc in range(tt // LANES):
            cols = slice(lc * LANES, (lc + 1) * LANES)
            g = jnp.zeros((PEER_KEYS, LANES), F32)
            for h in range(PEER_HEADS):
                c = s1_ref[h, a:a + 1, cols] + s2_ref[h, :, cols]
                w = p1_ref[h, a:a + 1, cols] * p2_ref[h, :, cols]
                g = g + jnp.where(c >= thr_ref[pl.ds(h, 1), cols], w, 0.0)
            blk = a_t[a * PEER_KEYS:(a + 1) * PEER_KEYS, cols]
            wt_sc[a * PEER_KEYS:(a + 1) * PEER_KEYS, cols] = (g * _gelu(blk)).astype(BF16)
    acc_sc[...] += _dot_tn(wt_sc[...], v_ref[...])

    @pl.when(i == pl.num_programs(1) - 1)
    def _():
        z = alpha * x_ref[...] + gate_ref[0] * acc_sc[...]
        o_ref[...] = _layer_norm(z, g_ref[...], b_ref[...])


def _peer_experts(xs, modv, layer, n_lat_tiles, n_out_tiles, ln_g, ln_b, u, v, route, alpha):
    tp, d = xs.shape
    tt = TOKEN_TILE
    te = 8 * PEER_KEYS
    n_exp = u.shape[0]
    s1, s2, p1, p2, thr = route
    once = pl.Buffered(1)
    tok = lambda j, i: (j, 0)
    fixed = lambda j, i: (0, 0)
    mod = lambda chunk: (lambda j, i: _mod_index(layer, chunk, n_lat_tiles)(j))
    rt = lambda: pl.BlockSpec((PEER_HEADS, PEER_KEYS, tt), lambda j, i: (0, 0, j), pipeline_mode=once)
    rt1 = lambda: pl.BlockSpec((PEER_HEADS, te // PEER_KEYS, tt), lambda j, i: (0, i, j))
    return pl.pallas_call(
        functools.partial(_peer_expert_kernel, alpha=alpha, te=te, tt=tt),
        grid=(n_out_tiles, n_exp // te),
        in_specs=[
            pl.BlockSpec((tt, d), tok, pipeline_mode=once),
            pl.BlockSpec((1, 1, d), mod(4)),
            pl.BlockSpec((1, 1, d), mod(3)),
            pl.BlockSpec((1, 1, d), mod(5)),
            pl.BlockSpec((1, d), fixed),
            pl.BlockSpec((1, d), fixed),
            pl.BlockSpec((te, d), lambda j, i: (i, 0)),
            pl.BlockSpec((te, d), lambda j, i: (i, 0)),
            rt1(), rt(), rt1(), rt(),
            pl.BlockSpec((PEER_HEADS, tt), lambda j, i: (0, j)),
        ],
        out_specs=pl.BlockSpec((tt, d), tok),
        out_shape=jax.ShapeDtypeStruct((n_out_tiles * tt, d), F32),
        scratch_shapes=[
            pltpu.VMEM((tt, d), BF16),
            pltpu.VMEM((te, tt), BF16),
            pltpu.VMEM((tt, d), F32),
        ],
        compiler_params=_cparams(("arbitrary", "arbitrary")),
        name="peer_experts_ln",
    )(xs, modv, modv, modv, ln_g, ln_b, u, v, s1, s2, p1, p2, thr)


def _qk_permutation():
    perm = np.arange(N_COLBLOCKS * LANES)
    for blk in range(COL_AQ, COL_AV):
        base = blk * LANES
        for lane in range(LANES):
            b, m, i = lane // 64, (lane // 32) % 2, lane % 32
            perm[base + lane] = base + m * A_QK_DIM + 2 * i + b
    for blk in range(COL_BQ, COL_BV):
        base = blk * LANES
        for lane in range(LANES):
            b, i = lane // 64, lane % 64
            perm[base + lane] = base + 2 * i + b
    return perm


def _rope_tables(s_lat, tp):
    t = jnp.arange(s_lat)
    row = (t // GRID_W).astype(F32)
    col = (t % GRID_W).astype(F32)
    lane = np.arange(LANES)
    sign = jnp.asarray(np.where(lane < LANES // 2, -1.0, 1.0), F32)

    def tabs(dim, lane_to_pair):
        d_axis = dim // 2
        inv = ROPE_BASE ** (-jnp.arange(0, d_axis, 2, dtype=F32) / d_axis)
        ang = jnp.concatenate([row[:, None] * inv, col[:, None] * inv], axis=-1)
        ang = ang[:, lane_to_pair]
        pad = ((0, tp - s_lat), (0, 0))
        return (jnp.pad(jnp.cos(ang), pad, constant_values=1.0),
                jnp.pad(jnp.sin(ang) * sign, pad, constant_values=0.0))

    ca, sa = tabs(A_QK_DIM, lane % 32)
    cb, sb = tabs(HEAD_DIM, lane % 64)
    return jnp.stack([ca, cb]), jnp.stack([sa, sb])


def kernel(x, c, ctx, c_ctx, w_ada, b_ada, w_in, w_out, beta_out, ln1_g, ln1_b, ln2_g, ln2_b, diff_lq1, diff_lk1, diff_lq2, diff_lk2, diff_subln, sink, na_rpb, peer_wq, peer_keys, peer_u, peer_v):
    depth = w_ada.shape[0]
    _, s_lat, d = x.shape
    n_ctx = ctx.shape[1]
    assert x.shape[0] == 1 and s_lat % TOKEN_TILE == 0 and n_ctx == 256 and s_lat % n_ctx == 0
    n_lat_tiles = s_lat // TOKEN_TILE
    tp = (n_lat_tiles + 1) * TOKEN_TILE
    alpha = (2 * depth) ** 0.25

    xs = jnp.concatenate([x[0], ctx[0], jnp.zeros((tp - s_lat - n_ctx, d), x.dtype)], axis=0)
    cc = jnp.concatenate([c[0:1], c_ctx[None, :], jnp.zeros((6, d), c.dtype)], axis=0)
    mod = _ada_modulation(cc, w_ada, b_ada)
    modv = mod[:, :2, :].reshape(depth * 2 * 6, 1, d)
    perm = _qk_permutation()
    rope_tabs = _rope_tables(s_lat, tp)

    for l in range(depth):
        lam_init = 0.8 - 0.6 * math.exp(-0.3 * l)
        last = l == depth - 1
        w_in_l = w_in[l][:, perm].astype(BF16)
        proj = _projection(xs, modv, l, 1, 0, w_in_l, n_lat_tiles, rope_tabs)
        beta = beta_out[l][None, :]
        ma = _attention_a(proj, diff_lq1[l][None], diff_lk1[l][None], diff_lq2[l][None], diff_lk2[l][None],
                          diff_subln[l][None], beta, s_lat, n_ctx, lam_init)
        mb = _attention_b(proj, sink[l], beta, s_lat, n_ctx)
        mc = _attention_c(proj, _na_bias_tables(na_rpb[l], s_lat), beta, s_lat, n_ctx)
        xs = _out_projection(ma, mb, mc, w_out[l].astype(BF16), xs, modv, l, n_lat_tiles,
                             ln1_g[l][None], ln1_b[l][None], alpha)
        qp = _projection(xs, modv, l, 4, 3, peer_wq[l].astype(BF16), n_lat_tiles)
        route = _peer_route(qp, peer_keys[l].astype(BF16))
        xs = _peer_experts(xs, modv, l, n_lat_tiles, n_lat_tiles if last else n_lat_tiles + 1,
                           ln2_g[l][None], ln2_b[l][None], peer_u[l].astype(BF16), peer_v[l].astype(BF16),
                           route, alpha)
    return xs[None]
```

```python
import functools
import math

import numpy as np
import jax
import jax.numpy as jnp
from jax import lax
from jax.experimental import pallas as pl
from jax.experimental.pallas import tpu as pltpu

F32 = jnp.float32
BF16 = jnp.bfloat16

GRID_W = 64
HEAD_DIM = 128
A_HEADS = 4
B_HEADS = 8
B_KV_HEADS = 2
B_GROUP = 4
C_HEADS = 4
A_QK_DIM = 64
WINDOW = 128
NA_KH = 8
NA_KW = 16
PEER_HEADS = 8
PEER_KEYS = 128
PEER_TOPK = 16
ROPE_BASE = 10000.0
LN_EPS = 1e-5

LANES = 128
TOKEN_TILE = 512
KEY_CHUNK = 256
A_ONES_ROWS = 16
A_Q_SCALE = A_QK_DIM ** -0.5 * math.log2(math.e)
VMEM_LIMIT = 56 * 1024 * 1024

COL_AQ, COL_AK, COL_AV = 0, 4, 8
COL_BQ, COL_BK, COL_BV = 12, 20, 22
COL_CQ, COL_CK, COL_CV = 24, 28, 32
N_COLBLOCKS = 36


def _cparams(sem):
    return pltpu.CompilerParams(dimension_semantics=sem, vmem_limit_bytes=VMEM_LIMIT)


def _dot_nt(a, b):
    return lax.dot_general(a, b, (((1,), (1,)), ((), ())), preferred_element_type=F32)


def _dot_tn(a, b):
    return lax.dot_general(a, b, (((0,), (0,)), ((), ())), preferred_element_type=F32)


def _layer_norm(y, g, b):
    mu = jnp.mean(y, axis=-1, keepdims=True)
    d = y - mu
    var = jnp.mean(d * d, axis=-1, keepdims=True)
    return d * lax.rsqrt(var + LN_EPS) * g + b


def _ada_kernel(c_ref, w_ref, b_ref, o_ref):
    c = c_ref[...]
    a = c * jax.nn.sigmoid(c)
    o_ref[0] = jnp.dot(a.astype(BF16), w_ref[0].astype(BF16), preferred_element_type=F32) + b_ref[0]


def _ada_modulation(cc, w_ada, b_ada):
    depth, d, n = w_ada.shape
    tn = 1024
    return pl.pallas_call(
        _ada_kernel,
        grid=(depth, n // tn),
        in_specs=[
            pl.BlockSpec((8, d), lambda l, j: (0, 0)),
            pl.BlockSpec((1, d, tn), lambda l, j: (l, 0, j)),
            pl.BlockSpec((1, 1, tn), lambda l, j: (l, 0, j)),
        ],
        out_specs=pl.BlockSpec((1, 8, tn), lambda l, j: (l, 0, j)),
        out_shape=jax.ShapeDtypeStruct((depth, 8, n), F32),
        compiler_params=_cparams(("arbitrary", "arbitrary")),
        name="ada_modulation",
    )(cc, w_ada, b_ada.reshape(depth, 1, n))


def _rope_kind(colblock):
    if colblock < COL_AV:
        return 0
    if COL_BQ <= colblock < COL_BV:
        return 1
    return None


def _proj_kernel(x_ref, sc_ref, sh_ref, w_ref, *rest, tn, rope):
    if rope:
        cos_ref, sin_ref, o_ref, vt_ref = rest
    else:
        (o_ref,) = rest
    xm = (x_ref[...] * (1.0 + sc_ref[0]) + sh_ref[0]).astype(BF16)
    n = w_ref.shape[1]
    for jn in range(n // tn):
        y = jnp.dot(xm, w_ref[:, jn * tn:(jn + 1) * tn], preferred_element_type=F32)
        for jb in range(tn // LANES):
            colblock = jn * (tn // LANES) + jb
            yb = y[:, jb * LANES:(jb + 1) * LANES]
            kind = _rope_kind(colblock) if rope else None
            if kind is not None:
                yb = yb * cos_ref[kind] + pltpu.roll(yb, LANES // 2, 1) * sin_ref[kind]
            if rope and colblock < COL_AK:
                yb = yb * A_Q_SCALE
            o_ref[:, colblock * LANES:(colblock + 1) * LANES] = yb.astype(o_ref.dtype)
            if rope and COL_AV <= colblock < COL_BQ:
                hd = colblock - COL_AV
                for ck in range(yb.shape[0] // KEY_CHUNK):
                    vt_ref[ck, hd * HEAD_DIM:(hd + 1) * HEAD_DIM, :] = (
                        yb[ck * KEY_CHUNK:(ck + 1) * KEY_CHUNK, :].T.astype(vt_ref.dtype))


def _mod_index(layer_slot, chunk, n_lat_tiles):
    def index(i):
        who = jnp.where(i >= n_lat_tiles, 1, 0)
        return ((layer_slot * 2 + who) * 6 + chunk, 0, 0)
    return index


def _projection(xs, modv, layer, chunk_scale, chunk_shift, w, n_lat_tiles, rope_tabs=None):
    tp, d = xs.shape
    n = w.shape[1]
    tm = TOKEN_TILE
    rope = rope_tabs is not None
    in_specs = [
        pl.BlockSpec((tm, d), lambda i: (i, 0)),
        pl.BlockSpec((1, 1, d), _mod_index(layer, chunk_scale, n_lat_tiles)),
        pl.BlockSpec((1, 1, d), _mod_index(layer, chunk_shift, n_lat_tiles)),
        pl.BlockSpec((d, n), lambda i: (0, 0), pipeline_mode=pl.Buffered(1)),
    ]
    args = [xs, modv, modv, w]
    if rope:
        in_specs += [pl.BlockSpec((2, tm, LANES), lambda i: (0, i, 0))] * 2
        args += list(rope_tabs)
    out_specs = pl.BlockSpec((tm, n), lambda i: (i, 0))
    out_shape = jax.ShapeDtypeStruct((tp, n), BF16)
    if rope:
        per = tm // KEY_CHUNK
        out_specs = [out_specs, pl.BlockSpec((per, A_HEADS * HEAD_DIM, KEY_CHUNK), lambda i: (i, 0, 0))]
        out_shape = [out_shape, jax.ShapeDtypeStruct((tp // KEY_CHUNK, A_HEADS * HEAD_DIM, KEY_CHUNK), BF16)]
    return pl.pallas_call(
        functools.partial(_proj_kernel, tn=512, rope=rope),
        grid=(tp // tm,),
        in_specs=in_specs,
        out_specs=out_specs,
        out_shape=out_shape,
        compiler_params=_cparams(("arbitrary",)),
        name="mod_projection_rope" if rope else "mod_projection",
    )(*args)


def _attn_a_kernel(lq1_ref, lk1_ref, lq2_ref, lk2_ref, subln_ref, beta_ref, q_ref, k_ref, vt_ref, o_ref,
                   qs_sc, s0_sc, s1_sc, m_sc, acc_sc, *, s_lat, tq, n_sub, n_lat_tiles, lam_init):
    i = pl.program_id(1)
    lane = lax.broadcasted_iota(jnp.int32, (1, LANES), 1)
    map1 = ((lane // 32) % 2) == 1
    q = q_ref[...]
    zero = jnp.zeros_like(q)
    qs_sc[0:tq, :] = jnp.where(map1, zero, q)
    qs_sc[tq:2 * tq, :] = jnp.where(map1, q, zero)
    m_sc[...] = jnp.full_like(m_sc, -jnp.inf)
    acc_sc[...] = jnp.zeros_like(acc_sc)
    ones_rows = jnp.where(lax.broadcasted_iota(jnp.int32, (A_ONES_ROWS, KEY_CHUNK), 0) == 0, 1.0, 0.0).astype(BF16)

    def keys(c):
        start = c * KEY_CHUNK
        if not isinstance(c, int):
            start = pl.multiple_of(start, KEY_CHUNK)
        return k_ref[pl.ds(start, KEY_CHUNK), :]

    def scores(group, s_sc):
        for j in range(n_sub):
            s_sc[j] = _dot_nt(keys(group * n_sub + j), qs_sc[...])

    def update(chunks, s):
        m_prev = m_sc[...]
        m_new = m_prev
        for sj in s:
            m_new = jnp.maximum(m_new, jnp.max(sj, axis=0, keepdims=True))
        acc = jnp.exp2(m_prev - m_new) * acc_sc[...]
        for c, sj in zip(chunks, s):
            p = jnp.exp2(sj - m_new).astype(BF16)
            v1 = jnp.concatenate([vt_ref[c], ones_rows], axis=0)
            acc = acc + jnp.dot(v1, p, preferred_element_type=F32)
        acc_sc[...] = acc
        m_sc[...] = m_new

    ctx_chunk = s_lat // KEY_CHUNK
    update([ctx_chunk], [_dot_nt(keys(ctx_chunk), qs_sc[...])])

    @pl.when(i < n_lat_tiles)
    def _():
        n_groups = s_lat // (KEY_CHUNK * n_sub)
        scores(0, s0_sc)

        def body(g2, carry):
            g = 2 * g2
            scores(g + 1, s1_sc)
            update([g * n_sub + j for j in range(n_sub)], [s0_sc[j] for j in range(n_sub)])
            scores(jnp.minimum(g + 2, n_groups - 1), s0_sc)
            update([(g + 1) * n_sub + j for j in range(n_sub)], [s1_sc[j] for j in range(n_sub)])
            return carry
        lax.fori_loop(0, n_groups // 2, body, 0)

    lam = (jnp.exp(jnp.sum(lq1_ref[...] * lk1_ref[...], axis=-1, keepdims=True))
           - jnp.exp(jnp.sum(lq2_ref[...] * lk2_ref[...], axis=-1, keepdims=True)) + lam_init)
    l = acc_sc[HEAD_DIM:HEAD_DIM + 1, :]
    o_t = (acc_sc[0:HEAD_DIM, 0:tq] / l[:, 0:tq] - lam * (acc_sc[0:HEAD_DIM, tq:2 * tq] / l[:, tq:2 * tq]))
    o = o_t.T
    y = o * lax.rsqrt(jnp.mean(o * o, axis=-1, keepdims=True) + LN_EPS) * subln_ref[...]
    o_ref[...] = (y * (1.0 - lam_init) * beta_ref[...]).astype(o_ref.dtype)


def _attention_a(proj, vt, lq1, lk1, lq2, lk2, subln, beta, s_lat, n_ctx, lam_init):
    tp = proj.shape[0]
    assert n_ctx == KEY_CHUNK
    tq, n_sub = 256, 2
    n_lat_tiles = s_lat // tq
    vec = lambda: pl.BlockSpec((1, A_QK_DIM), lambda h, i: (0, 0))
    return pl.pallas_call(
        functools.partial(_attn_a_kernel, s_lat=s_lat, tq=tq, n_sub=n_sub, n_lat_tiles=n_lat_tiles,
                          lam_init=lam_init),
        grid=(A_HEADS, tp // tq),
        in_specs=[
            vec(), vec(), vec(), vec(),
            pl.BlockSpec((1, HEAD_DIM), lambda h, i: (0, 0)),
            pl.BlockSpec((1, HEAD_DIM), lambda h, i: (0, h)),
            pl.BlockSpec((tq, HEAD_DIM), lambda h, i: (i, COL_AQ + h)),
            pl.BlockSpec((tp, HEAD_DIM), lambda h, i: (0, COL_AK + h)),
            pl.BlockSpec((tp // KEY_CHUNK, HEAD_DIM, KEY_CHUNK), lambda h, i: (0, h, 0)),
        ],
        out_specs=pl.BlockSpec((tq, HEAD_DIM), lambda h, i: (i, h)),
        out_shape=jax.ShapeDtypeStruct((tp, A_HEADS * HEAD_DIM), BF16),
        scratch_shapes=[
            pltpu.VMEM((2 * tq, HEAD_DIM), BF16),
            pltpu.VMEM((n_sub, KEY_CHUNK, 2 * tq), F32),
            pltpu.VMEM((n_sub, KEY_CHUNK, 2 * tq), F32),
            pltpu.VMEM((1, 2 * tq), F32),
            pltpu.VMEM((HEAD_DIM + A_ONES_ROWS, 2 * tq), F32),
        ],
        compiler_params=_cparams(("arbitrary", "arbitrary")),
        name="attn_diff",
    )(lq1, lk1, lq2, lk2, subln, beta, proj, proj, vt)


def _attn_b_kernel(sink_ref, beta_ref, q_ref, kp_ref, kc_ref, kn_ref, kx_ref, vp_ref, vc_ref, vn_ref, vx_ref,
                   o_ref, *, s_lat, tq):
    kv = pl.program_id(0)
    i = pl.program_id(1)
    scale = HEAD_DIM ** -0.5
    q0 = i * tq
    qpos = q0 + lax.broadcasted_iota(jnp.int32, (tq, 1), 0)

    def valid(kstart, n):
        kpos = kstart + lax.broadcasted_iota(jnp.int32, (1, n), 1)
        return (jnp.abs(kpos - qpos) <= WINDOW) & (kpos >= 0) & (kpos < s_lat) & (qpos < s_lat)

    ok_p = valid(q0 - WINDOW, WINDOW)
    ok_c = valid(q0, tq)
    ok_n = valid(q0 + tq, WINDOW)
    neg = -jnp.inf
    for g in range(B_GROUP):
        qg = q_ref[:, g * HEAD_DIM:(g + 1) * HEAD_DIM]
        s_x = _dot_nt(qg, kx_ref[...]) * scale
        s_p = jnp.where(ok_p, _dot_nt(qg, kp_ref[...]) * scale, neg)
        s_c = jnp.where(ok_c, _dot_nt(qg, kc_ref[...]) * scale, neg)
        s_n = jnp.where(ok_n, _dot_nt(qg, kn_ref[...]) * scale, neg)
        snk = sink_ref[kv * B_GROUP + g]
        mx = jnp.maximum(jnp.maximum(jnp.max(s_x, -1, keepdims=True), jnp.max(s_p, -1, keepdims=True)),
                         jnp.maximum(jnp.max(s_c, -1, keepdims=True), jnp.max(s_n, -1, keepdims=True)))
        mx = jnp.maximum(mx, snk)
        e_x = jnp.exp(s_x - mx)
        e_p = jnp.exp(s_p - mx)
        e_c = jnp.exp(s_c - mx)
        e_n = jnp.exp(s_n - mx)
        den = (jnp.exp(snk - mx) + jnp.sum(e_x, -1, keepdims=True) + jnp.sum(e_p, -1, keepdims=True)
               + jnp.sum(e_c, -1, keepdims=True) + jnp.sum(e_n, -1, keepdims=True))
        o = (jnp.dot(e_x.astype(BF16), vx_ref[...], preferred_element_type=F32)
             + jnp.dot(e_p.astype(BF16), vp_ref[...], preferred_element_type=F32)
             + jnp.dot(e_c.astype(BF16), vc_ref[...], preferred_element_type=F32)
             + jnp.dot(e_n.astype(BF16), vn_ref[...], preferred_element_type=F32))
        o_ref[:, g * HEAD_DIM:(g + 1) * HEAD_DIM] = (
            (o / den) * beta_ref[:, g * HEAD_DIM:(g + 1) * HEAD_DIM]).astype(o_ref.dtype)


def _attention_b(proj, sink, beta, s_lat, n_ctx):
    tp = proj.shape[0]
    tq = 256
    per = tq // WINDOW
    last_halo = tp // WINDOW - 1
    gw = B_GROUP * HEAD_DIM

    def halo_prev(col):
        return pl.BlockSpec((WINDOW, HEAD_DIM), lambda kv, i: (jnp.maximum(i * per - 1, 0), col + kv))

    def halo_next(col):
        return pl.BlockSpec((WINDOW, HEAD_DIM), lambda kv, i: (jnp.minimum((i + 1) * per, last_halo), col + kv))

    def cur(col):
        return pl.BlockSpec((tq, HEAD_DIM), lambda kv, i: (i, col + kv))

    def ctx(col):
        return pl.BlockSpec((n_ctx, HEAD_DIM), lambda kv, i: (s_lat // n_ctx, col + kv))

    return pl.pallas_call(
        functools.partial(_attn_b_kernel, s_lat=s_lat, tq=tq),
        grid=(B_KV_HEADS, tp // tq),
        in_specs=[
            pl.BlockSpec(memory_space=pltpu.SMEM),
            pl.BlockSpec((1, gw), lambda kv, i: (0, (A_HEADS * HEAD_DIM) // gw + kv)),
            pl.BlockSpec((tq, gw), lambda kv, i: (i, (COL_BQ * LANES) // gw + kv)),
            halo_prev(COL_BK), cur(COL_BK), halo_next(COL_BK), ctx(COL_BK),
            halo_prev(COL_BV), cur(COL_BV), halo_next(COL_BV), ctx(COL_BV),
        ],
        out_specs=pl.BlockSpec((tq, gw), lambda kv, i: (i, kv)),
        out_shape=jax.ShapeDtypeStruct((tp, B_HEADS * HEAD_DIM), BF16),
        compiler_params=_cparams(("arbitrary", "arbitrary")),
        name="attn_window",
    )(sink, beta, proj, proj, proj, proj, proj, proj, proj, proj, proj)


def _na_bias_tables(rpb, s_lat):
    rows = s_lat // GRID_W
    tr = TOKEN_TILE // GRID_W
    kh = min(NA_KH, rows)

    neg = -1e30
    n_heads = rpb.shape[0]
    per_c = []
    for c in range(GRID_W):
        c0 = min(max(c - NA_KW // 2, 0), GRID_W - NA_KW)
        lo = c0 - c + NA_KW - 1
        per_c.append(jnp.pad(rpb[:, :, lo:lo + NA_KW], ((0, 0), (0, 0), (c0, GRID_W - NA_KW - c0)),
                             constant_values=neg))
    colb = jnp.stack(per_c, axis=2).astype(F32)
    off = jnp.full((n_heads, GRID_W, GRID_W), neg, F32)

    def table(q_row0, n_rows):
        per_r = []
        for r in range(q_row0, q_row0 + tr):
            r0 = min(max(r - kh // 2, 0), n_rows - kh)
            blocks = []
            for kr in range(q_row0 - tr, q_row0 + 2 * tr):
                inside = r0 <= kr < r0 + kh and 0 <= kr < n_rows
                blocks.append(colb[:, kr - r + NA_KH - 1] if inside else off)
            per_r.append(jnp.concatenate(blocks, axis=-1))
        return jnp.stack(per_r, axis=1).reshape(n_heads, TOKEN_TILE, 3 * TOKEN_TILE)

    far = 4 * tr
    out = [table(0, rows), table(far, 2 * far + tr), table(rows - tr, rows)]
    out.append(jnp.full_like(out[0], neg))
    return jnp.stack(out, axis=0)


def _attn_c_kernel(beta_ref, bias_ref, q_ref, kp_ref, kc_ref, kn_ref, kx_ref, vp_ref, vc_ref, vn_ref, vx_ref,
                   o_ref, *, tq):
    scale = HEAD_DIM ** -0.5
    q = q_ref[...]
    s_x = _dot_nt(q, kx_ref[...]) * scale
    s_p = _dot_nt(q, kp_ref[...]) * scale + bias_ref[0, 0, :, 0:tq]
    s_c = _dot_nt(q, kc_ref[...]) * scale + bias_ref[0, 0, :, tq:2 * tq]
    s_n = _dot_nt(q, kn_ref[...]) * scale + bias_ref[0, 0, :, 2 * tq:3 * tq]
    mx = jnp.maximum(jnp.maximum(jnp.max(s_x, -1, keepdims=True), jnp.max(s_p, -1, keepdims=True)),
                     jnp.maximum(jnp.max(s_c, -1, keepdims=True), jnp.max(s_n, -1, keepdims=True)))
    e_x = jnp.exp(s_x - mx)
    e_p = jnp.exp(s_p - mx)
    e_c = jnp.exp(s_c - mx)
    e_n = jnp.exp(s_n - mx)
    den = (jnp.sum(e_x, -1, keepdims=True) + jnp.sum(e_p, -1, keepdims=True)
           + jnp.sum(e_c, -1, keepdims=True) + jnp.sum(e_n, -1, keepdims=True))
    o = (jnp.dot(e_x.astype(BF16), vx_ref[...], preferred_element_type=F32)
         + jnp.dot(e_p.astype(BF16), vp_ref[...], preferred_element_type=F32)
         + jnp.dot(e_c.astype(BF16), vc_ref[...], preferred_element_type=F32)
         + jnp.dot(e_n.astype(BF16), vn_ref[...], preferred_element_type=F32))
    o_ref[...] = ((o / den) * beta_ref[...]).astype(o_ref.dtype)


def _attention_c(proj, bias_tabs, beta, s_lat, n_ctx):
    tp = proj.shape[0]
    tq = TOKEN_TILE
    n_lat_tiles = s_lat // tq
    n_tiles = tp // tq
    beta_col0 = (A_HEADS + B_HEADS)

    def kind(i):
        return jnp.where(i >= n_lat_tiles, 3, jnp.where(i == 0, 0, jnp.where(i == n_lat_tiles - 1, 2, 1)))

    def prev(col):
        return pl.BlockSpec((tq, HEAD_DIM), lambda h, i: (jnp.maximum(i - 1, 0), col + h))

    def cur(col):
        return pl.BlockSpec((tq, HEAD_DIM), lambda h, i: (i, col + h))

    def nxt(col):
        return pl.BlockSpec((tq, HEAD_DIM), lambda h, i: (jnp.minimum(i + 1, n_tiles - 1), col + h))

    def ctx(col):
        return pl.BlockSpec((n_ctx, HEAD_DIM), lambda h, i: (s_lat // n_ctx, col + h))

    return pl.pallas_call(
        functools.partial(_attn_c_kernel, tq=tq),
        grid=(C_HEADS, n_tiles),
        in_specs=[
            pl.BlockSpec((1, HEAD_DIM), lambda h, i: (0, beta_col0 + h)),
            pl.BlockSpec((1, 1, tq, 3 * tq), lambda h, i: (kind(i), h, 0, 0)),
            cur(COL_CQ),
            prev(COL_CK), cur(COL_CK), nxt(COL_CK), ctx(COL_CK),
            prev(COL_CV), cur(COL_CV), nxt(COL_CV), ctx(COL_CV),
        ],
        out_specs=pl.BlockSpec((tq, HEAD_DIM), lambda h, i: (i, h)),
        out_shape=jax.ShapeDtypeStruct((tp, C_HEADS * HEAD_DIM), BF16),
        compiler_params=_cparams(("arbitrary", "arbitrary")),
        name="attn_neighbourhood",
    )(beta, bias_tabs, proj, proj, proj, proj, proj, proj, proj, proj, proj)


def _outproj_kernel(ma_ref, mb_ref, mc_ref, w_ref, x_ref, gate_ref, g_ref, b_ref, o_ref, *, alpha):
    wa = A_HEADS * HEAD_DIM
    wb = wa + B_HEADS * HEAD_DIM
    y = (jnp.dot(ma_ref[...], w_ref[0:wa, :], preferred_element_type=F32)
         + jnp.dot(mb_ref[...], w_ref[wa:wb, :], preferred_element_type=F32)
         + jnp.dot(mc_ref[...], w_ref[wb:, :], preferred_element_type=F32))
    z = alpha * x_ref[...] + gate_ref[0] * y
    o_ref[...] = _layer_norm(z, g_ref[...], b_ref[...])


def _out_projection(ma, mb, mc, w_out, xs, modv, layer, n_lat_tiles, ln_g, ln_b, alpha):
    tp, d = xs.shape
    tm = TOKEN_TILE // 2
    n_lat = n_lat_tiles * (TOKEN_TILE // tm)
    row = lambda i: (i, 0)
    fixed = lambda i: (0, 0)
    return pl.pallas_call(
        functools.partial(_outproj_kernel, alpha=alpha),
        grid=(tp // tm,),
        in_specs=[
            pl.BlockSpec((tm, ma.shape[1]), row),
            pl.BlockSpec((tm, mb.shape[1]), row),
            pl.BlockSpec((tm, mc.shape[1]), row),
            pl.BlockSpec(w_out.shape, fixed, pipeline_mode=pl.Buffered(1)),
            pl.BlockSpec((tm, d), row),
            pl.BlockSpec((1, 1, d), _mod_index(layer, 2, n_lat)),
            pl.BlockSpec((1, d), fixed),
            pl.BlockSpec((1, d), fixed),
        ],
        out_specs=pl.BlockSpec((tm, d), row),
        out_shape=jax.ShapeDtypeStruct((tp, d), F32),
        compiler_params=_cparams(("arbitrary",)),
        name="out_projection_ln",
    )(ma, mb, mc, w_out, xs, modv, ln_g, ln_b)


def _take_top(vals, rounds):
    tops, cnts = [], []
    for _ in range(rounds):
        m = jnp.max(vals, axis=0, keepdims=True)
        eq = vals == m
        tops.append(m)
        cnts.append(jnp.sum(jnp.where(eq, 1.0, 0.0), axis=0, keepdims=True))
        vals = jnp.where(eq, -jnp.inf, vals)
    return jnp.concatenate(tops, axis=0), jnp.concatenate(cnts, axis=0)


def _route_kernel(q_ref, keys_ref, s1_ref, s2_ref, p1_ref, p2_ref, thr_ref):
    k = PEER_TOPK
    for h in range(PEER_HEADS):
        s1 = _dot_nt(keys_ref[h, 0], q_ref[:, (2 * h) * LANES:(2 * h + 1) * LANES])
        s2 = _dot_nt(keys_ref[h, 1], q_ref[:, (2 * h + 1) * LANES:(2 * h + 2) * LANES])
        v1, n1 = _take_top(s1, k)
        v2, n2 = _take_top(s2, k)
        t = s1.shape[1]
        cand = jnp.concatenate([v1[r:r + 1] + v2 for r in range(k)], axis=0)
        mult = jnp.concatenate([n1[r:r + 1] * n2 for r in range(k)], axis=0)
        thr = jnp.full((1, t), jnp.inf, F32)
        seen = jnp.zeros((1, t), F32)
        rem = cand
        for _ in range(k):
            m = jnp.max(rem, axis=0, keepdims=True)
            eq = rem == m
            thr = jnp.where(seen < k, m, thr)
            seen = seen + jnp.sum(jnp.where(eq, mult, 0.0), axis=0, keepdims=True)
            rem = jnp.where(eq, -jnp.inf, rem)
        e1 = jnp.exp(v1 - v1[0:1])
        e2 = jnp.exp(v2 - v2[0:1])
        pair = jnp.concatenate([e1[r:r + 1] * e2 for r in range(k)], axis=0)
        z = jnp.sum(jnp.where(cand >= thr, mult * pair, 0.0), axis=0, keepdims=True)
        s1_ref[h] = s1
        s2_ref[h] = s2
        p1_ref[h] = jnp.exp(s1 - v1[0:1])
        p2_ref[h] = jnp.exp(s2 - v2[0:1]) / z
        thr_ref[pl.ds(h, 1), :] = thr


def _peer_route(qp, keys):
    tp = qp.shape[0]
    tt = TOKEN_TILE
    big = pl.BlockSpec((PEER_HEADS, PEER_KEYS, tt), lambda j: (0, 0, j))
    shape = jax.ShapeDtypeStruct((PEER_HEADS, PEER_KEYS, tp), F32)
    return pl.pallas_call(
        _route_kernel,
        grid=(tp // tt,),
        in_specs=[
            pl.BlockSpec((tt, qp.shape[1]), lambda j: (j, 0)),
            pl.BlockSpec(keys.shape, lambda j: (0, 0, 0, 0)),
        ],
        out_specs=[big, big, big, big, pl.BlockSpec((PEER_HEADS, tt), lambda j: (0, j))],
        out_shape=[shape, shape, shape, shape, jax.ShapeDtypeStruct((PEER_HEADS, tp), F32)],
        compiler_params=_cparams(("arbitrary",)),
        name="peer_route",
    )(qp, keys)


def _gelu(a):
    return 0.5 * a * (1.0 + lax.erf(a * (2.0 ** -0.5)))


def _peer_expert_kernel(x_ref, sc_ref, sh_ref, gate_ref, g_ref, b_ref, u_ref, v_ref,
                        s1_ref, s2_ref, p1_ref, p2_ref, thr_ref, o_ref, ux_sc, wt_sc, acc_sc,
                        *, alpha, te, tt):
    i = pl.program_id(1)

    @pl.when(i == 0)
    def _():
        ux_sc[...] = (x_ref[...] * (1.0 + sc_ref[0]) + sh_ref[0]).astype(BF16)
        acc_sc[...] = jnp.zeros_like(acc_sc)

    a_t = _dot_nt(u_ref[...], ux_sc[...])
    for a in range(te // PEER_KEYS):
        for lc in range(tt // LANES):
            cols = slice(lc * LANES, (lc + 1) * LANES)
            g = jnp.zeros((PEER_KEYS, LANES), F32)
            for h in range(PEER_HEADS):
                c = s1_ref[h, a:a + 1, cols] + s2_ref[h, :, cols]
                w = p1_ref[h, a:a + 1, cols] * p2_ref[h, :, cols]
                g = g + jnp.where(c >= thr_ref[pl.ds(h, 1), cols], w, 0.0)
            blk = a_t[a * PEER_KEYS:(a + 1) * PEER_KEYS, cols]
            wt_sc[a * PEER_KEYS:(a + 1) * PEER_KEYS, cols] = (g * _gelu(blk)).astype(BF16)
    acc_sc[...] += _dot_tn(wt_sc[...], v_ref[...])

    @pl.when(i == pl.num_programs(1) - 1)
    def _():
        z = alpha * x_ref[...] + gate_ref[0] * acc_sc[...]
        o_ref[...] = _layer_norm(z, g_ref[...], b_ref[...])


def _peer_experts(xs, modv, layer, n_lat_tiles, n_out_tiles, ln_g, ln_b, u, v, route, alpha):
    tp, d = xs.shape
    tt = TOKEN_TILE
    te = 8 * PEER_KEYS
    n_exp = u.shape[0]
    s1, s2, p1, p2, thr = route
    once = pl.Buffered(1)
    tok = lambda j, i: (j, 0)
    fixed = lambda j, i: (0, 0)
    mod = lambda chunk: (lambda j, i: _mod_index(layer, chunk, n_lat_tiles)(j))
    rt = lambda: pl.BlockSpec((PEER_HEADS, PEER_KEYS, tt), lambda j, i: (0, 0, j), pipeline_mode=once)
    rt1 = lambda: pl.BlockSpec((PEER_HEADS, te // PEER_KEYS, tt), lambda j, i: (0, i, j))
    return pl.pallas_call(
        functools.partial(_peer_expert_kernel, alpha=alpha, te=te, tt=tt),
        grid=(n_out_tiles, n_exp // te),
        in_specs=[
            pl.BlockSpec((tt, d), tok, pipeline_mode=once),
            pl.BlockSpec((1, 1, d), mod(4)),
            pl.BlockSpec((1, 1, d), mod(3)),
            pl.BlockSpec((1, 1, d), mod(5)),
            pl.BlockSpec((1, d), fixed),
            pl.BlockSpec((1, d), fixed),
            pl.BlockSpec((te, d), lambda j, i: (i, 0)),
            pl.BlockSpec((te, d), lambda j, i: (i, 0)),
            rt1(), rt(), rt1(), rt(),
            pl.BlockSpec((PEER_HEADS, tt), lambda j, i: (0, j)),
        ],
        out_specs=pl.BlockSpec((tt, d), tok),
        out_shape=jax.ShapeDtypeStruct((n_out_tiles * tt, d), F32),
        scratch_shapes=[
            pltpu.VMEM((tt, d), BF16),
            pltpu.VMEM((te, tt), BF16),
            pltpu.VMEM((tt, d), F32),
        ],
        compiler_params=_cparams(("arbitrary", "arbitrary")),
        name="peer_experts_ln",
    )(xs, modv, modv, modv, ln_g, ln_b, u, v, s1, s2, p1, p2, thr)


def _permute_qk_columns(w):
    d = w.shape[0]
    a_end, b_start, b_end = COL_AV * LANES, COL_BQ * LANES, COL_BV * LANES
    wa = w[:, :a_end].reshape(d, COL_AV, 2, A_QK_DIM // 2, 2)
    wa = wa.transpose(0, 1, 4, 2, 3).reshape(d, a_end)
    wb = w[:, b_start:b_end].reshape(d, COL_BV - COL_BQ, HEAD_DIM // 2, 2)
    wb = wb.transpose(0, 1, 3, 2).reshape(d, b_end - b_start)
    return jnp.concatenate([wa, w[:, a_end:b_start], wb, w[:, b_end:]], axis=1)


def _rope_tables(s_lat, tp):
    t = jnp.arange(s_lat)
    row = (t // GRID_W).astype(F32)
    col = (t % GRID_W).astype(F32)
    lane = np.arange(LANES)
    sign = jnp.asarray(np.where(lane < LANES // 2, -1.0, 1.0), F32)

    def tabs(dim, lane_to_pair):
        d_axis = dim // 2
        inv = ROPE_BASE ** (-jnp.arange(0, d_axis, 2, dtype=F32) / d_axis)
        ang = jnp.concatenate([row[:, None] * inv, col[:, None] * inv], axis=-1)
        ang = ang[:, lane_to_pair]
        pad = ((0, tp - s_lat), (0, 0))
        return (jnp.pad(jnp.cos(ang), pad, constant_values=1.0),
                jnp.pad(jnp.sin(ang) * sign, pad, constant_values=0.0))

    ca, sa = tabs(A_QK_DIM, lane % 32)
    cb, sb = tabs(HEAD_DIM, lane % 64)
    return jnp.stack([ca, cb]), jnp.stack([sa, sb])


def kernel(x, c, ctx, c_ctx, w_ada, b_ada, w_in, w_out, beta_out, ln1_g, ln1_b, ln2_g, ln2_b, diff_lq1, diff_lk1, diff_lq2, diff_lk2, diff_subln, sink, na_rpb, peer_wq, peer_keys, peer_u, peer_v):
    depth = w_ada.shape[0]
    _, s_lat, d = x.shape
    n_ctx = ctx.shape[1]
    assert x.shape[0] == 1 and s_lat % TOKEN_TILE == 0 and n_ctx == 256 and s_lat % n_ctx == 0
    n_lat_tiles = s_lat // TOKEN_TILE
    tp = (n_lat_tiles + 1) * TOKEN_TILE
    alpha = (2 * depth) ** 0.25

    xs = jnp.concatenate([x[0], ctx[0], jnp.zeros((tp - s_lat - n_ctx, d), x.dtype)], axis=0)
    cc = jnp.concatenate([c[0:1], c_ctx[None, :], jnp.zeros((6, d), c.dtype)], axis=0)
    mod = _ada_modulation(cc, w_ada, b_ada)
    modv = mod[:, :2, :].reshape(depth * 2 * 6, 1, d)
    rope_tabs = _rope_tables(s_lat, tp)

    for l in range(depth):
        lam_init = 0.8 - 0.6 * math.exp(-0.3 * l)
        last = l == depth - 1
        w_in_l = _permute_qk_columns(w_in[l].astype(BF16))
        proj, vt = _projection(xs, modv, l, 1, 0, w_in_l, n_lat_tiles, rope_tabs)
        beta = beta_out[l][None, :]
        ma = _attention_a(proj, vt, diff_lq1[l][None], diff_lk1[l][None], diff_lq2[l][None], diff_lk2[l][None],
                          diff_subln[l][None], beta, s_lat, n_ctx, lam_init)
        mb = _attention_b(proj, sink[l], beta, s_lat, n_ctx)
        mc = _attention_c(proj, _na_bias_tables(na_rpb[l], s_lat), beta, s_lat, n_ctx)
        xs = _out_projection(ma, mb, mc, w_out[l].astype(BF16), xs, modv, l, n_lat_tiles,
                             ln1_g[l][None], ln1_b[l][None], alpha)
        qp = _projection(xs, modv, l, 4, 3, peer_wq[l].astype(BF16), n_lat_tiles)
        route = _peer_route(qp, peer_keys[l].astype(BF16))
        xs = _peer_experts(xs, modv, l, n_lat_tiles, n_lat_tiles if last else n_lat_tiles + 1,
                           ln2_g[l][None], ln2_b[l][None], peer_u[l].astype(BF16), peer_v[l].astype(BF16),
                           route, alpha)
    return xs[None]
```

```python
import functools
import math

import numpy as np
import jax
import jax.numpy as jnp
from jax import lax
from jax.experimental import pallas as pl
from jax.experimental.pallas import tpu as pltpu

F32 = jnp.float32
BF16 = jnp.bfloat16

GRID_W = 64
HEAD_DIM = 128
A_HEADS = 4
B_HEADS = 8
B_KV_HEADS = 2
B_GROUP = 4
C_HEADS = 4
A_QK_DIM = 64
WINDOW = 128
NA_KH = 8
NA_KW = 16
PEER_HEADS = 8
PEER_KEYS = 128
PEER_TOPK = 16
ROPE_BASE = 10000.0
LN_EPS = 1e-5

LANES = 128
BF16_SUBLANES = 16
TOKEN_TILE = 512
KEY_CHUNK = 256
A_ONES_ROWS = 16
A_Q_SCALE = A_QK_DIM ** -0.5 * math.log2(math.e)
VMEM_LIMIT = 56 * 1024 * 1024

COL_AQ, COL_AK, COL_AV = 0, 4, 8
COL_BQ, COL_BK, COL_BV = 12, 20, 22
COL_CQ, COL_CK, COL_CV = 24, 28, 32
N_COLBLOCKS = 36


def _cparams(sem):
    return pltpu.CompilerParams(dimension_semantics=sem, vmem_limit_bytes=VMEM_LIMIT)


def _dot_nt(a, b):
    return lax.dot_general(a, b, (((1,), (1,)), ((), ())), preferred_element_type=F32)


def _dot_tn(a, b):
    return lax.dot_general(a, b, (((0,), (0,)), ((), ())), preferred_element_type=F32)


def _layer_norm(y, g, b):
    mu = jnp.mean(y, axis=-1, keepdims=True)
    d = y - mu
    var = jnp.mean(d * d, axis=-1, keepdims=True)
    return d * lax.rsqrt(var + LN_EPS) * g + b


def _ada_kernel(c_ref, w_ref, b_ref, o_ref):
    c = c_ref[...]
    a = c * jax.nn.sigmoid(c)
    o_ref[0] = jnp.dot(a.astype(BF16), w_ref[0].astype(BF16), preferred_element_type=F32) + b_ref[0]


def _ada_modulation(cc, w_ada, b_ada):
    depth, d, n = w_ada.shape
    tn = 1024
    return pl.pallas_call(
        _ada_kernel,
        grid=(depth, n // tn),
        in_specs=[
            pl.BlockSpec((8, d), lambda l, j: (0, 0)),
            pl.BlockSpec((1, d, tn), lambda l, j: (l, 0, j)),
            pl.BlockSpec((1, 1, tn), lambda l, j: (l, 0, j)),
        ],
        out_specs=pl.BlockSpec((1, 8, tn), lambda l, j: (l, 0, j)),
        out_shape=jax.ShapeDtypeStruct((depth, 8, n), F32),
        compiler_params=_cparams(("arbitrary", "arbitrary")),
        name="ada_modulation",
    )(cc, w_ada, b_ada.reshape(depth, 1, n))


def _rope_kind(colblock):
    if colblock < COL_AV:
        return 0
    if COL_BQ <= colblock < COL_BV:
        return 1
    return None


def _proj_kernel(x_ref, sc_ref, sh_ref, w_ref, *rest, tn, rope):
    if rope:
        cos_ref, sin_ref, o_ref, vt_ref = rest
    else:
        (o_ref,) = rest
    xm = (x_ref[...] * (1.0 + sc_ref[0]) + sh_ref[0]).astype(BF16)
    n = w_ref.shape[1]
    for jn in range(n // tn):
        y = jnp.dot(xm, w_ref[:, jn * tn:(jn + 1) * tn], preferred_element_type=F32)
        for jb in range(tn // LANES):
            colblock = jn * (tn // LANES) + jb
            yb = y[:, jb * LANES:(jb + 1) * LANES]
            kind = _rope_kind(colblock) if rope else None
            if kind is not None:
                yb = yb * cos_ref[kind] + pltpu.roll(yb, LANES // 2, 1) * sin_ref[kind]
            if rope and colblock < COL_AK:
                yb = yb * A_Q_SCALE
            o_ref[:, colblock * LANES:(colblock + 1) * LANES] = yb.astype(o_ref.dtype)
            if rope and COL_AV <= colblock < COL_BQ:
                hd = colblock - COL_AV
                for ck in range(yb.shape[0] // KEY_CHUNK):
                    vt_ref[ck, hd * HEAD_DIM:(hd + 1) * HEAD_DIM, :] = (
                        yb[ck * KEY_CHUNK:(ck + 1) * KEY_CHUNK, :].T.astype(vt_ref.dtype))


def _mod_index(layer_slot, chunk, n_lat_tiles):
    def index(i):
        who = jnp.where(i >= n_lat_tiles, 1, 0)
        return ((layer_slot * 2 + who) * 6 + chunk, 0, 0)
    return index


def _projection(xs, modv, layer, chunk_scale, chunk_shift, w, n_lat_tiles, rope_tabs=None):
    tp, d = xs.shape
    n = w.shape[1]
    tm = TOKEN_TILE
    rope = rope_tabs is not None
    in_specs = [
        pl.BlockSpec((tm, d), lambda i: (i, 0)),
        pl.BlockSpec((1, 1, d), _mod_index(layer, chunk_scale, n_lat_tiles)),
        pl.BlockSpec((1, 1, d), _mod_index(layer, chunk_shift, n_lat_tiles)),
        pl.BlockSpec((d, n), lambda i: (0, 0), pipeline_mode=pl.Buffered(1)),
    ]
    args = [xs, modv, modv, w]
    if rope:
        in_specs += [pl.BlockSpec((2, tm, LANES), lambda i: (0, i, 0))] * 2
        args += list(rope_tabs)
    out_specs = pl.BlockSpec((tm, n), lambda i: (i, 0))
    out_shape = jax.ShapeDtypeStruct((tp, n), BF16)
    if rope:
        per = tm // KEY_CHUNK
        out_specs = [out_specs, pl.BlockSpec((per, A_HEADS * HEAD_DIM, KEY_CHUNK), lambda i: (i, 0, 0))]
        out_shape = [out_shape, jax.ShapeDtypeStruct((tp // KEY_CHUNK, A_HEADS * HEAD_DIM, KEY_CHUNK), BF16)]
    return pl.pallas_call(
        functools.partial(_proj_kernel, tn=512, rope=rope),
        grid=(tp // tm,),
        in_specs=in_specs,
        out_specs=out_specs,
        out_shape=out_shape,
        compiler_params=_cparams(("arbitrary",)),
        name="mod_projection_rope" if rope else "mod_projection",
    )(*args)


def _attn_a_kernel(lq1_ref, lk1_ref, lq2_ref, lk2_ref, subln_ref, beta_ref, q_ref, k_ref, vt_ref, o_ref,
                   qs_sc, s0_sc, s1_sc, m_sc, acc_sc, *, s_lat, tq, n_sub, n_lat_tiles, lam_init):
    i = pl.program_id(1)
    lane = lax.broadcasted_iota(jnp.int32, (1, LANES), 1)
    map1 = ((lane // 32) % 2) == 1
    q = q_ref[...]
    zero = jnp.zeros_like(q)
    qs_sc[0:tq, :] = jnp.where(map1, zero, q)
    qs_sc[tq:2 * tq, :] = jnp.where(map1, q, zero)
    m_sc[...] = jnp.full_like(m_sc, -jnp.inf)
    acc_sc[...] = jnp.zeros_like(acc_sc)
    ones_rows = jnp.where(lax.broadcasted_iota(jnp.int32, (A_ONES_ROWS, KEY_CHUNK), 0) == 0, 1.0, 0.0).astype(BF16)

    def keys(c):
        start = c * KEY_CHUNK
        if not isinstance(c, int):
            start = pl.multiple_of(start, KEY_CHUNK)
        return k_ref[pl.ds(start, KEY_CHUNK), :]

    def scores(group, s_sc):
        for j in range(n_sub):
            s_sc[j] = _dot_nt(keys(group * n_sub + j), qs_sc[...])

    def update(chunks, s):
        m_prev = m_sc[...]
        m_new = m_prev
        for sj in s:
            m_new = jnp.maximum(m_new, jnp.max(sj, axis=0, keepdims=True))
        acc = jnp.exp2(m_prev - m_new) * acc_sc[...]
        for c, sj in zip(chunks, s):
            p = jnp.exp2(sj - m_new).astype(BF16)
            v1 = jnp.concatenate([vt_ref[c], ones_rows], axis=0)
            acc = acc + jnp.dot(v1, p, preferred_element_type=F32)
        acc_sc[...] = acc
        m_sc[...] = m_new

    ctx_chunk = s_lat // KEY_CHUNK
    update([ctx_chunk], [_dot_nt(keys(ctx_chunk), qs_sc[...])])

    @pl.when(i < n_lat_tiles)
    def _():
        n_groups = s_lat // (KEY_CHUNK * n_sub)
        scores(0, s0_sc)

        def body(g2, carry):
            g = 2 * g2
            scores(g + 1, s1_sc)
            update([g * n_sub + j for j in range(n_sub)], [s0_sc[j] for j in range(n_sub)])
            scores(jnp.minimum(g + 2, n_groups - 1), s0_sc)
            update([(g + 1) * n_sub + j for j in range(n_sub)], [s1_sc[j] for j in range(n_sub)])
            return carry
        lax.fori_loop(0, n_groups // 2, body, 0)

    lam = (jnp.exp(jnp.sum(lq1_ref[...] * lk1_ref[...], axis=-1, keepdims=True))
           - jnp.exp(jnp.sum(lq2_ref[...] * lk2_ref[...], axis=-1, keepdims=True)) + lam_init)
    l = acc_sc[HEAD_DIM:HEAD_DIM + 1, :]
    o_t = (acc_sc[0:HEAD_DIM, 0:tq] / l[:, 0:tq] - lam * (acc_sc[0:HEAD_DIM, tq:2 * tq] / l[:, tq:2 * tq]))
    o = o_t.T
    y = o * lax.rsqrt(jnp.mean(o * o, axis=-1, keepdims=True) + LN_EPS) * subln_ref[...]
    o_ref[...] = (y * (1.0 - lam_init) * beta_ref[...]).astype(o_ref.dtype)


def _attention_a(proj, vt, lq1, lk1, lq2, lk2, subln, beta, s_lat, n_ctx, lam_init):
    tp = proj.shape[0]
    assert n_ctx == KEY_CHUNK
    tq, n_sub = 256, 2
    n_lat_tiles = s_lat // tq
    vec = lambda: pl.BlockSpec((1, A_QK_DIM), lambda h, i: (0, 0))
    return pl.pallas_call(
        functools.partial(_attn_a_kernel, s_lat=s_lat, tq=tq, n_sub=n_sub, n_lat_tiles=n_lat_tiles,
                          lam_init=lam_init),
        grid=(A_HEADS, tp // tq),
        in_specs=[
            vec(), vec(), vec(), vec(),
            pl.BlockSpec((1, HEAD_DIM), lambda h, i: (0, 0)),
            pl.BlockSpec((1, HEAD_DIM), lambda h, i: (0, h)),
            pl.BlockSpec((tq, HEAD_DIM), lambda h, i: (i, COL_AQ + h)),
            pl.BlockSpec((tp, HEAD_DIM), lambda h, i: (0, COL_AK + h)),
            pl.BlockSpec((tp // KEY_CHUNK, HEAD_DIM, KEY_CHUNK), lambda h, i: (0, h, 0)),
        ],
        out_specs=pl.BlockSpec((tq, HEAD_DIM), lambda h, i: (i, h)),
        out_shape=jax.ShapeDtypeStruct((tp, A_HEADS * HEAD_DIM), BF16),
        scratch_shapes=[
            pltpu.VMEM((2 * tq, HEAD_DIM), BF16),
            pltpu.VMEM((n_sub, KEY_CHUNK, 2 * tq), F32),
            pltpu.VMEM((n_sub, KEY_CHUNK, 2 * tq), F32),
            pltpu.VMEM((1, 2 * tq), F32),
            pltpu.VMEM((HEAD_DIM + A_ONES_ROWS, 2 * tq), F32),
        ],
        compiler_params=_cparams(("arbitrary", "arbitrary")),
        name="attn_diff",
    )(lq1, lk1, lq2, lk2, subln, beta, proj, proj, vt)


def _attn_b_kernel(sink_ref, beta_ref, q_ref, kp_ref, kc_ref, kn_ref, kx_ref, vp_ref, vc_ref, vn_ref, vx_ref,
                   o_ref, *, s_lat, tq):
    kv = pl.program_id(0)
    i = pl.program_id(1)
    scale = HEAD_DIM ** -0.5
    q0 = i * tq
    qpos = q0 + lax.broadcasted_iota(jnp.int32, (tq, 1), 0)

    def valid(kstart, n):
        kpos = kstart + lax.broadcasted_iota(jnp.int32, (1, n), 1)
        return (jnp.abs(kpos - qpos) <= WINDOW) & (kpos >= 0) & (kpos < s_lat) & (qpos < s_lat)

    ok_p = valid(q0 - WINDOW, WINDOW)
    ok_c = valid(q0, tq)
    ok_n = valid(q0 + tq, WINDOW)
    neg = -jnp.inf
    for g in range(B_GROUP):
        qg = q_ref[:, g * HEAD_DIM:(g + 1) * HEAD_DIM]
        s_x = _dot_nt(qg, kx_ref[...]) * scale
        s_p = jnp.where(ok_p, _dot_nt(qg, kp_ref[...]) * scale, neg)
        s_c = jnp.where(ok_c, _dot_nt(qg, kc_ref[...]) * scale, neg)
        s_n = jnp.where(ok_n, _dot_nt(qg, kn_ref[...]) * scale, neg)
        snk = sink_ref[kv * B_GROUP + g]
        mx = jnp.maximum(jnp.maximum(jnp.max(s_x, -1, keepdims=True), jnp.max(s_p, -1, keepdims=True)),
                         jnp.maximum(jnp.max(s_c, -1, keepdims=True), jnp.max(s_n, -1, keepdims=True)))
        mx = jnp.maximum(mx, snk)
        e_x = jnp.exp(s_x - mx)
        e_p = jnp.exp(s_p - mx)
        e_c = jnp.exp(s_c - mx)
        e_n = jnp.exp(s_n - mx)
        den = (jnp.exp(snk - mx) + jnp.sum(e_x, -1, keepdims=True) + jnp.sum(e_p, -1, keepdims=True)
               + jnp.sum(e_c, -1, keepdims=True) + jnp.sum(e_n, -1, keepdims=True))
        o = (jnp.dot(e_x.astype(BF16), vx_ref[...], preferred_element_type=F32)
             + jnp.dot(e_p.astype(BF16), vp_ref[...], preferred_element_type=F32)
             + jnp.dot(e_c.astype(BF16), vc_ref[...], preferred_element_type=F32)
             + jnp.dot(e_n.astype(BF16), vn_ref[...], preferred_element_type=F32))
        o_ref[:, g * HEAD_DIM:(g + 1) * HEAD_DIM] = (
            (o / den) * beta_ref[:, g * HEAD_DIM:(g + 1) * HEAD_DIM]).astype(o_ref.dtype)


def _attention_b(proj, sink, beta, s_lat, n_ctx):
    tp = proj.shape[0]
    tq = 256
    per = tq // WINDOW
    last_halo = tp // WINDOW - 1
    gw = B_GROUP * HEAD_DIM

    def halo_prev(col):
        return pl.BlockSpec((WINDOW, HEAD_DIM), lambda kv, i: (jnp.maximum(i * per - 1, 0), col + kv))

    def halo_next(col):
        return pl.BlockSpec((WINDOW, HEAD_DIM), lambda kv, i: (jnp.minimum((i + 1) * per, last_halo), col + kv))

    def cur(col):
        return pl.BlockSpec((tq, HEAD_DIM), lambda kv, i: (i, col + kv))

    def ctx(col):
        return pl.BlockSpec((n_ctx, HEAD_DIM), lambda kv, i: (s_lat // n_ctx, col + kv))

    return pl.pallas_call(
        functools.partial(_attn_b_kernel, s_lat=s_lat, tq=tq),
        grid=(B_KV_HEADS, tp // tq),
        in_specs=[
            pl.BlockSpec(memory_space=pltpu.SMEM),
            pl.BlockSpec((1, gw), lambda kv, i: (0, (A_HEADS * HEAD_DIM) // gw + kv)),
            pl.BlockSpec((tq, gw), lambda kv, i: (i, (COL_BQ * LANES) // gw + kv)),
            halo_prev(COL_BK), cur(COL_BK), halo_next(COL_BK), ctx(COL_BK),
            halo_prev(COL_BV), cur(COL_BV), halo_next(COL_BV), ctx(COL_BV),
        ],
        out_specs=pl.BlockSpec((tq, gw), lambda kv, i: (i, kv)),
        out_shape=jax.ShapeDtypeStruct((tp, B_HEADS * HEAD_DIM), BF16),
        compiler_params=_cparams(("arbitrary", "arbitrary")),
        name="attn_window",
    )(sink, beta, proj, proj, proj, proj, proj, proj, proj, proj, proj)


def _na_bias_tables(rpb, s_lat):
    rows = s_lat // GRID_W
    tr = TOKEN_TILE // GRID_W
    kh = min(NA_KH, rows)

    neg = -1e30
    n_heads = rpb.shape[0]
    per_c = []
    for c in range(GRID_W):
        c0 = min(max(c - NA_KW // 2, 0), GRID_W - NA_KW)
        lo = c0 - c + NA_KW - 1
        per_c.append(jnp.pad(rpb[:, :, lo:lo + NA_KW], ((0, 0), (0, 0), (c0, GRID_W - NA_KW - c0)),
                             constant_values=neg))
    colb = jnp.stack(per_c, axis=2).astype(F32)
    off = jnp.full((n_heads, GRID_W, GRID_W), neg, F32)

    def table(q_row0, n_rows):
        per_r = []
        for r in range(q_row0, q_row0 + tr):
            r0 = min(max(r - kh // 2, 0), n_rows - kh)
            blocks = []
            for kr in range(q_row0 - tr, q_row0 + 2 * tr):
                inside = r0 <= kr < r0 + kh and 0 <= kr < n_rows
                blocks.append(colb[:, kr - r + NA_KH - 1] if inside else off)
            per_r.append(jnp.concatenate(blocks, axis=-1))
        return jnp.stack(per_r, axis=1).reshape(n_heads, TOKEN_TILE, 3 * TOKEN_TILE)

    far = 4 * tr
    out = [table(0, rows), table(far, 2 * far + tr), table(rows - tr, rows)]
    out.append(jnp.full_like(out[0], neg))
    return jnp.stack(out, axis=0)


def _attn_c_kernel(beta_ref, bias_ref, q_ref, kp_ref, kc_ref, kn_ref, kx_ref, vp_ref, vc_ref, vn_ref, vx_ref,
                   o_ref, *, tq):
    scale = HEAD_DIM ** -0.5
    q = q_ref[...]
    s_x = _dot_nt(q, kx_ref[...]) * scale
    s_p = _dot_nt(q, kp_ref[...]) * scale + bias_ref[0, 0, :, 0:tq]
    s_c = _dot_nt(q, kc_ref[...]) * scale + bias_ref[0, 0, :, tq:2 * tq]
    s_n = _dot_nt(q, kn_ref[...]) * scale + bias_ref[0, 0, :, 2 * tq:3 * tq]
    mx = jnp.maximum(jnp.maximum(jnp.max(s_x, -1, keepdims=True), jnp.max(s_p, -1, keepdims=True)),
                     jnp.maximum(jnp.max(s_c, -1, keepdims=True), jnp.max(s_n, -1, keepdims=True)))
    e_x = jnp.exp(s_x - mx)
    e_p = jnp.exp(s_p - mx)
    e_c = jnp.exp(s_c - mx)
    e_n = jnp.exp(s_n - mx)
    den = (jnp.sum(e_x, -1, keepdims=True) + jnp.sum(e_p, -1, keepdims=True)
           + jnp.sum(e_c, -1, keepdims=True) + jnp.sum(e_n, -1, keepdims=True))
    o = (jnp.dot(e_x.astype(BF16), vx_ref[...], preferred_element_type=F32)
         + jnp.dot(e_p.astype(BF16), vp_ref[...], preferred_element_type=F32)
         + jnp.dot(e_c.astype(BF16), vc_ref[...], preferred_element_type=F32)
         + jnp.dot(e_n.astype(BF16), vn_ref[...], preferred_element_type=F32))
    o_ref[...] = ((o / den) * beta_ref[...]).astype(o_ref.dtype)


def _attention_c(proj, bias_tabs, beta, s_lat, n_ctx):
    tp = proj.shape[0]
    tq = TOKEN_TILE
    n_lat_tiles = s_lat // tq
    n_tiles = tp // tq
    beta_col0 = (A_HEADS + B_HEADS)

    def kind(i):
        return jnp.where(i >= n_lat_tiles, 3, jnp.where(i == 0, 0, jnp.where(i == n_lat_tiles - 1, 2, 1)))

    def prev(col):
        return pl.BlockSpec((tq, HEAD_DIM), lambda h, i: (jnp.maximum(i - 1, 0), col + h))

    def cur(col):
        return pl.BlockSpec((tq, HEAD_DIM), lambda h, i: (i, col + h))

    def nxt(col):
        return pl.BlockSpec((tq, HEAD_DIM), lambda h, i: (jnp.minimum(i + 1, n_tiles - 1), col + h))

    def ctx(col):
        return pl.BlockSpec((n_ctx, HEAD_DIM), lambda h, i: (s_lat // n_ctx, col + h))

    return pl.pallas_call(
        functools.partial(_attn_c_kernel, tq=tq),
        grid=(C_HEADS, n_tiles),
        in_specs=[
            pl.BlockSpec((1, HEAD_DIM), lambda h, i: (0, beta_col0 + h)),
            pl.BlockSpec((1, 1, tq, 3 * tq), lambda h, i: (kind(i), h, 0, 0)),
            cur(COL_CQ),
            prev(COL_CK), cur(COL_CK), nxt(COL_CK), ctx(COL_CK),
            prev(COL_CV), cur(COL_CV), nxt(COL_CV), ctx(COL_CV),
        ],
        out_specs=pl.BlockSpec((tq, HEAD_DIM), lambda h, i: (i, h)),
        out_shape=jax.ShapeDtypeStruct((tp, C_HEADS * HEAD_DIM), BF16),
        compiler_params=_cparams(("arbitrary", "arbitrary")),
        name="attn_neighbourhood",
    )(beta, bias_tabs, proj, proj, proj, proj, proj, proj, proj, proj, proj)


def _outproj_kernel(ma_ref, mb_ref, mc_ref, w_ref, x_ref, gate_ref, g_ref, b_ref, o_ref, *, alpha):
    wa = A_HEADS * HEAD_DIM
    wb = wa + B_HEADS * HEAD_DIM
    y = (jnp.dot(ma_ref[...], w_ref[0:wa, :], preferred_element_type=F32)
         + jnp.dot(mb_ref[...], w_ref[wa:wb, :], preferred_element_type=F32)
         + jnp.dot(mc_ref[...], w_ref[wb:, :], preferred_element_type=F32))
    z = alpha * x_ref[...] + gate_ref[0] * y
    o_ref[...] = _layer_norm(z, g_ref[...], b_ref[...])


def _out_projection(ma, mb, mc, w_out, xs, modv, layer, n_lat_tiles, ln_g, ln_b, alpha):
    tp, d = xs.shape
    tm = TOKEN_TILE // 2
    n_lat = n_lat_tiles * (TOKEN_TILE // tm)
    row = lambda i: (i, 0)
    fixed = lambda i: (0, 0)
    return pl.pallas_call(
        functools.partial(_outproj_kernel, alpha=alpha),
        grid=(tp // tm,),
        in_specs=[
            pl.BlockSpec((tm, ma.shape[1]), row),
            pl.BlockSpec((tm, mb.shape[1]), row),
            pl.BlockSpec((tm, mc.shape[1]), row),
            pl.BlockSpec(w_out.shape, fixed, pipeline_mode=pl.Buffered(1)),
            pl.BlockSpec((tm, d), row),
            pl.BlockSpec((1, 1, d), _mod_index(layer, 2, n_lat)),
            pl.BlockSpec((1, d), fixed),
            pl.BlockSpec((1, d), fixed),
        ],
        out_specs=pl.BlockSpec((tm, d), row),
        out_shape=jax.ShapeDtypeStruct((tp, d), F32),
        compiler_params=_cparams(("arbitrary",)),
        name="out_projection_ln",
    )(ma, mb, mc, w_out, xs, modv, ln_g, ln_b)


def _take_top(vals, rounds):
    tops, cnts = [], []
    for _ in range(rounds):
        m = jnp.max(vals, axis=0, keepdims=True)
        eq = vals == m
        tops.append(m)
        cnts.append(jnp.sum(jnp.where(eq, 1.0, 0.0), axis=0, keepdims=True))
        vals = jnp.where(eq, -jnp.inf, vals)
    return jnp.concatenate(tops, axis=0), jnp.concatenate(cnts, axis=0)


def _route_kernel(q_ref, keys_ref, n1_ref, rk2_ref, p1_ref, p2_ref):
    k = PEER_TOPK
    for h in range(PEER_HEADS):
        s1 = _dot_nt(keys_ref[h, 0], q_ref[:, (2 * h) * LANES:(2 * h + 1) * LANES])
        s2 = _dot_nt(keys_ref[h, 1], q_ref[:, (2 * h + 1) * LANES:(2 * h + 2) * LANES])
        v1, n1 = _take_top(s1, k)
        v2, n2 = _take_top(s2, k)
        t = s1.shape[1]
        cand = jnp.concatenate([v1[r:r + 1] + v2 for r in range(k)], axis=0)
        mult = jnp.concatenate([n1[r:r + 1] * n2 for r in range(k)], axis=0)
        thr = jnp.full((1, t), jnp.inf, F32)
        seen = jnp.zeros((1, t), F32)
        rem = cand
        for _ in range(k):
            m = jnp.max(rem, axis=0, keepdims=True)
            eq = rem == m
            thr = jnp.where(seen < k, m, thr)
            seen = seen + jnp.sum(jnp.where(eq, mult, 0.0), axis=0, keepdims=True)
            rem = jnp.where(eq, -jnp.inf, rem)
        e1 = jnp.exp(v1 - v1[0:1])
        e2 = jnp.exp(v2 - v2[0:1])
        pair = jnp.concatenate([e1[r:r + 1] * e2 for r in range(k)], axis=0)
        z = jnp.sum(jnp.where(cand >= thr, mult * pair, 0.0), axis=0, keepdims=True)
        n1 = jnp.zeros_like(s1)
        rk2 = jnp.zeros_like(s2)
        for r in range(k):
            n1 = n1 + jnp.where(s1 + v2[r:r + 1] >= thr, n2[r:r + 1], 0.0)
            rk2 = rk2 + jnp.where(v2[r:r + 1] > s2, n2[r:r + 1], 0.0)
        n1_ref[h] = n1
        rk2_ref[h] = rk2.astype(BF16)
        p1_ref[h] = jnp.exp(s1 - v1[0:1])
        p2_ref[h] = (jnp.exp(s2 - v2[0:1]) / z).astype(BF16)


def _peer_route(qp, keys):
    tp = qp.shape[0]
    tt = TOKEN_TILE
    big = pl.BlockSpec((PEER_HEADS, PEER_KEYS, tt), lambda j: (0, 0, j))
    wide = jax.ShapeDtypeStruct((PEER_HEADS, PEER_KEYS, tp), F32)
    narrow = jax.ShapeDtypeStruct((PEER_HEADS, PEER_KEYS, tp), BF16)
    return pl.pallas_call(
        _route_kernel,
        grid=(tp // tt,),
        in_specs=[
            pl.BlockSpec((tt, qp.shape[1]), lambda j: (j, 0)),
            pl.BlockSpec(keys.shape, lambda j: (0, 0, 0, 0)),
        ],
        out_specs=[big, big, big, big],
        out_shape=[wide, narrow, wide, narrow],
        compiler_params=_cparams(("arbitrary",)),
        name="peer_route",
    )(qp, keys)


def _gelu(a):
    return 0.5 * a * (1.0 + lax.erf(a * (2.0 ** -0.5)))


def _peer_expert_kernel(x_ref, sc_ref, sh_ref, gate_ref, g_ref, b_ref, u_ref, v_ref,
                        n1_ref, rk2_ref, p1_ref, p2_ref, o_ref, ux_sc, w0_sc, w1_sc, at_sc, acc_sc,
                        *, alpha, te, tt, n_e):
    i = pl.program_id(1)
    pack = BF16_SUBLANES
    d = acc_sc.shape[1]
    n_blk = te // PEER_KEYS

    @pl.when(i == 0)
    def _():
        ux_sc[...] = (x_ref[...] * (1.0 + sc_ref[0]) + sh_ref[0]).astype(BF16)
        acc_sc[...] = jnp.zeros_like(acc_sc)
        w1_sc[...] = jnp.zeros_like(w1_sc)

    zero = jnp.zeros((pack, LANES), BF16)

    eg = at_sc.shape[1] // PEER_KEYS
    tg = at_sc.shape[2] // LANES
    n_eg, n_tg = n_blk // eg, tt // LANES // tg
    n_pieces = n_eg * n_tg
    dn = d // n_pieces

    def hidden(r):
        ge, gt = divmod(r, n_tg)
        at_sc[r % 2] = _dot_nt(u_ref[ge * eg * PEER_KEYS:(ge + 1) * eg * PEER_KEYS, :],
                               ux_sc[gt * tg * LANES:(gt + 1) * tg * LANES, :])

    def cell(cur_sc, r, ca, cl):
        ge, gt = divmod(r, n_tg)
        a, lc = ge * eg + ca, gt * tg + cl
        cols = slice(lc * LANES, (lc + 1) * LANES)
        a_t = at_sc.at[r % 2, ca * PEER_KEYS:(ca + 1) * PEER_KEYS, cl * LANES:(cl + 1) * LANES]
        n1 = [jnp.broadcast_to(n1_ref[h, a:a + 1, cols], (pack, LANES)).astype(BF16) for h in range(PEER_HEADS)]
        p1 = [jnp.broadcast_to(p1_ref[h, a:a + 1, cols], (pack, LANES)).astype(BF16) for h in range(PEER_HEADS)]
        pieces = []
        for rg in range(PEER_KEYS // pack):
            rows = slice(rg * pack, (rg + 1) * pack)
            g = zero
            for h in range(PEER_HEADS):
                w = p1[h] * p2_ref[h, rows, cols]
                g = g + jnp.where(rk2_ref[h, rows, cols] < n1[h], w, zero)
            pieces.append(g * _gelu(a_t[rows, :]).astype(BF16))
        blk = jnp.concatenate(pieces, axis=0)
        cur_sc[cols, a * PEER_KEYS:(a + 1) * PEER_KEYS] = blk.T

    def step(cur_sc, prev_sc):
        hidden(0)
        for r in range(n_pieces):
            cells = [(ca, cl) for ca in range(eg) for cl in range(tg)]
            half = len(cells) // 2
            if r + 1 < n_pieces:
                hidden(r + 1)
            for ca, cl in cells[:half]:
                cell(cur_sc, r, ca, cl)
            acc_sc[:, r * dn:(r + 1) * dn] += jnp.dot(prev_sc[...], v_ref[:, r * dn:(r + 1) * dn],
                                                      preferred_element_type=F32)
            for ca, cl in cells[half:]:
                cell(cur_sc, r, ca, cl)

    @pl.when((i < n_e) & (i % 2 == 0))
    def _():
        step(w0_sc, w1_sc)

    @pl.when((i < n_e) & (i % 2 == 1))
    def _():
        step(w1_sc, w0_sc)

    @pl.when(i == n_e)
    def _():
        last_sc = w1_sc if n_e % 2 == 0 else w0_sc
        y = acc_sc[...] + jnp.dot(last_sc[...], v_ref[...], preferred_element_type=F32)
        z = alpha * x_ref[...] + gate_ref[0] * y
        o_ref[...] = _layer_norm(z, g_ref[...], b_ref[...])


def _peer_experts(xs, modv, layer, n_lat_tiles, n_out_tiles, ln_g, ln_b, u, v, route, alpha):
    tp, d = xs.shape
    tt = TOKEN_TILE
    te = 8 * PEER_KEYS
    n_exp = u.shape[0]
    n1, rk2, p1, p2 = route
    once = pl.Buffered(1)
    tok = lambda j, i: (j, 0)
    fixed = lambda j, i: (0, 0)
    mod = lambda chunk: (lambda j, i: _mod_index(layer, chunk, n_lat_tiles)(j))
    rt = lambda: pl.BlockSpec((PEER_HEADS, PEER_KEYS, tt), lambda j, i: (0, 0, j), pipeline_mode=once)
    n_e = n_exp // te
    build = lambda i: jnp.minimum(i, n_e - 1)
    drain = lambda i: jnp.maximum(i - 1, 0)
    rt1 = lambda: pl.BlockSpec((PEER_HEADS, te // PEER_KEYS, tt), lambda j, i: (0, build(i), j))
    return pl.pallas_call(
        functools.partial(_peer_expert_kernel, alpha=alpha, te=te, tt=tt, n_e=n_e),
        grid=(n_out_tiles, n_e + 1),
        in_specs=[
            pl.BlockSpec((tt, d), tok, pipeline_mode=once),
            pl.BlockSpec((1, 1, d), mod(4)),
            pl.BlockSpec((1, 1, d), mod(3)),
            pl.BlockSpec((1, 1, d), mod(5)),
            pl.BlockSpec((1, d), fixed),
            pl.BlockSpec((1, d), fixed),
            pl.BlockSpec((te, d), lambda j, i: (build(i), 0)),
            pl.BlockSpec((te, d), lambda j, i: (drain(i), 0)),
            rt1(), rt(), rt1(), rt(),
        ],
        out_specs=pl.BlockSpec((tt, d), tok),
        out_shape=jax.ShapeDtypeStruct((n_out_tiles * tt, d), F32),
        scratch_shapes=[
            pltpu.VMEM((tt, d), BF16),
            pltpu.VMEM((tt, te), BF16),
            pltpu.VMEM((tt, te), BF16),
            pltpu.VMEM((2, 2 * PEER_KEYS, 2 * LANES), F32),
            pltpu.VMEM((tt, d), F32),
        ],
        compiler_params=_cparams(("arbitrary", "arbitrary")),
        name="peer_experts_ln",
    )(xs, modv, modv, modv, ln_g, ln_b, u, v, n1, rk2, p1, p2)


def _permute_qk_columns(w):
    d = w.shape[0]
    a_end, b_start, b_end = COL_AV * LANES, COL_BQ * LANES, COL_BV * LANES
    wa = w[:, :a_end].reshape(d, COL_AV, 2, A_QK_DIM // 2, 2)
    wa = wa.transpose(0, 1, 4, 2, 3).reshape(d, a_end)
    wb = w[:, b_start:b_end].reshape(d, COL_BV - COL_BQ, HEAD_DIM // 2, 2)
    wb = wb.transpose(0, 1, 3, 2).reshape(d, b_end - b_start)
    return jnp.concatenate([wa, w[:, a_end:b_start], wb, w[:, b_end:]], axis=1)


def _rope_tables(s_lat, tp):
    t = jnp.arange(s_lat)
    row = (t // GRID_W).astype(F32)
    col = (t % GRID_W).astype(F32)
    lane = np.arange(LANES)
    sign = jnp.asarray(np.where(lane < LANES // 2, -1.0, 1.0), F32)

    def tabs(dim, lane_to_pair):
        d_axis = dim // 2
        inv = ROPE_BASE ** (-jnp.arange(0, d_axis, 2, dtype=F32) / d_axis)
        ang = jnp.concatenate([row[:, None] * inv, col[:, None] * inv], axis=-1)
        ang = ang[:, lane_to_pair]
        pad = ((0, tp - s_lat), (0, 0))
        return (jnp.pad(jnp.cos(ang), pad, constant_values=1.0),
                jnp.pad(jnp.sin(ang) * sign, pad, constant_values=0.0))

    ca, sa = tabs(A_QK_DIM, lane % 32)
    cb, sb = tabs(HEAD_DIM, lane % 64)
    return jnp.stack([ca, cb]), jnp.stack([sa, sb])


def kernel(x, c, ctx, c_ctx, w_ada, b_ada, w_in, w_out, beta_out, ln1_g, ln1_b, ln2_g, ln2_b, diff_lq1, diff_lk1, diff_lq2, diff_lk2, diff_subln, sink, na_rpb, peer_wq, peer_keys, peer_u, peer_v):
    depth = w_ada.shape[0]
    _, s_lat, d = x.shape
    n_ctx = ctx.shape[1]
    assert x.shape[0] == 1 and s_lat % TOKEN_TILE == 0 and n_ctx == 256 and s_lat % n_ctx == 0
    n_lat_tiles = s_lat // TOKEN_TILE
    tp = (n_lat_tiles + 1) * TOKEN_TILE
    alpha = (2 * depth) ** 0.25

    xs = jnp.concatenate([x[0], ctx[0], jnp.zeros((tp - s_lat - n_ctx, d), x.dtype)], axis=0)
    cc = jnp.concatenate([c[0:1], c_ctx[None, :], jnp.zeros((6, d), c.dtype)], axis=0)
    mod = _ada_modulation(cc, w_ada, b_ada)
    modv = mod[:, :2, :].reshape(depth * 2 * 6, 1, d)
    rope_tabs = _rope_tables(s_lat, tp)

    for l in range(depth):
        lam_init = 0.8 - 0.6 * math.exp(-0.3 * l)
        last = l == depth - 1
        w_in_l = _permute_qk_columns(w_in[l].astype(BF16))
        proj, vt = _projection(xs, modv, l, 1, 0, w_in_l, n_lat_tiles, rope_tabs)
        beta = beta_out[l][None, :]
        ma = _attention_a(proj, vt, diff_lq1[l][None], diff_lk1[l][None], diff_lq2[l][None], diff_lk2[l][None],
                          diff_subln[l][None], beta, s_lat, n_ctx, lam_init)
        mb = _attention_b(proj, sink[l], beta, s_lat, n_ctx)
        mc = _attention_c(proj, _na_bias_tables(na_rpb[l], s_lat), beta, s_lat, n_ctx)
        xs = _out_projection(ma, mb, mc, w_out[l].astype(BF16), xs, modv, l, n_lat_tiles,
                             ln1_g[l][None], ln1_b[l][None], alpha)
        qp = _projection(xs, modv, l, 4, 3, peer_wq[l].astype(BF16), n_lat_tiles)
        route = _peer_route(qp, peer_keys[l].astype(BF16))
        xs = _peer_experts(xs, modv, l, n_lat_tiles, n_lat_tiles if last else n_lat_tiles + 1,
                           ln2_g[l][None], ln2_b[l][None], peer_u[l].astype(BF16), peer_v[l].astype(BF16),
                           route, alpha)
    return xs[None]
```

```python
import functools
import math

import numpy as np
import jax
import jax.numpy as jnp
from jax import lax
from jax.experimental import pallas as pl
from jax.experimental.pallas import tpu as pltpu

F32 = jnp.float32
BF16 = jnp.bfloat16

GRID_W = 64
HEAD_DIM = 128
A_HEADS = 4
B_HEADS = 8
B_KV_HEADS = 2
B_GROUP = 4
C_HEADS = 4
A_QK_DIM = 64
WINDOW = 128
NA_KH = 8
NA_KW = 16
PEER_HEADS = 8
PEER_KEYS = 128
PEER_TOPK = 16
ROPE_BASE = 10000.0
LN_EPS = 1e-5

LANES = 128
BF16_SUBLANES = 16
TOKEN_TILE = 512
KEY_CHUNK = 256
A_ONES_ROWS = 16
A_Q_SCALE = A_QK_DIM ** -0.5 * math.log2(math.e)
VMEM_LIMIT = 56 * 1024 * 1024

COL_AQ, COL_AK, COL_AV = 0, 4, 8
COL_BQ, COL_BK, COL_BV = 12, 20, 22
COL_CQ, COL_CK, COL_CV = 24, 28, 32
N_COLBLOCKS = 36


def _cparams(sem):
    return pltpu.CompilerParams(dimension_semantics=sem, vmem_limit_bytes=VMEM_LIMIT)


def _dot_nt(a, b):
    return lax.dot_general(a, b, (((1,), (1,)), ((), ())), preferred_element_type=F32)


def _dot_tn(a, b):
    return lax.dot_general(a, b, (((0,), (0,)), ((), ())), preferred_element_type=F32)


def _layer_norm(y, g, b):
    mu = jnp.mean(y, axis=-1, keepdims=True)
    d = y - mu
    var = jnp.mean(d * d, axis=-1, keepdims=True)
    return d * lax.rsqrt(var + LN_EPS) * g + b


def _ada_kernel(c_ref, w_ref, b_ref, o_ref):
    c = c_ref[...]
    a = c * jax.nn.sigmoid(c)
    o_ref[0] = jnp.dot(a.astype(BF16), w_ref[0].astype(BF16), preferred_element_type=F32) + b_ref[0]


def _ada_modulation(cc, w_ada, b_ada):
    depth, d, n = w_ada.shape
    tn = 1024
    return pl.pallas_call(
        _ada_kernel,
        grid=(depth, n // tn),
        in_specs=[
            pl.BlockSpec((8, d), lambda l, j: (0, 0)),
            pl.BlockSpec((1, d, tn), lambda l, j: (l, 0, j)),
            pl.BlockSpec((1, 1, tn), lambda l, j: (l, 0, j)),
        ],
        out_specs=pl.BlockSpec((1, 8, tn), lambda l, j: (l, 0, j)),
        out_shape=jax.ShapeDtypeStruct((depth, 8, n), F32),
        compiler_params=_cparams(("arbitrary", "arbitrary")),
        name="ada_modulation",
    )(cc, w_ada, b_ada.reshape(depth, 1, n))


def _rope_kind(colblock):
    if colblock < COL_AV:
        return 0
    if COL_BQ <= colblock < COL_BV:
        return 1
    return None


def _proj_kernel(x_ref, sc_ref, sh_ref, w_ref, *rest, tn, rope):
    if rope:
        cos_ref, sin_ref, o_ref, vt_ref = rest
    else:
        (o_ref,) = rest
    xm = (x_ref[...] * (1.0 + sc_ref[0]) + sh_ref[0]).astype(BF16)
    n = w_ref.shape[1]
    for jn in range(n // tn):
        y = jnp.dot(xm, w_ref[:, jn * tn:(jn + 1) * tn], preferred_element_type=F32)
        for jb in range(tn // LANES):
            colblock = jn * (tn // LANES) + jb
            yb = y[:, jb * LANES:(jb + 1) * LANES]
            kind = _rope_kind(colblock) if rope else None
            if kind is not None:
                yb = yb * cos_ref[kind] + pltpu.roll(yb, LANES // 2, 1) * sin_ref[kind]
            if rope and colblock < COL_AK:
                yb = yb * A_Q_SCALE
            o_ref[:, colblock * LANES:(colblock + 1) * LANES] = yb.astype(o_ref.dtype)
            if rope and COL_AV <= colblock < COL_BQ:
                hd = colblock - COL_AV
                for ck in range(yb.shape[0] // KEY_CHUNK):
                    vt_ref[ck, hd * HEAD_DIM:(hd + 1) * HEAD_DIM, :] = (
                        yb[ck * KEY_CHUNK:(ck + 1) * KEY_CHUNK, :].T.astype(vt_ref.dtype))


def _mod_index(layer_slot, chunk, n_lat_tiles):
    def index(i):
        who = jnp.where(i >= n_lat_tiles, 1, 0)
        return ((layer_slot * 2 + who) * 6 + chunk, 0, 0)
    return index


def _projection(xs, modv, layer, chunk_scale, chunk_shift, w, n_lat_tiles, rope_tabs=None):
    tp, d = xs.shape
    n = w.shape[1]
    tm = TOKEN_TILE
    rope = rope_tabs is not None
    in_specs = [
        pl.BlockSpec((tm, d), lambda i: (i, 0)),
        pl.BlockSpec((1, 1, d), _mod_index(layer, chunk_scale, n_lat_tiles)),
        pl.BlockSpec((1, 1, d), _mod_index(layer, chunk_shift, n_lat_tiles)),
        pl.BlockSpec((d, n), lambda i: (0, 0), pipeline_mode=pl.Buffered(1)),
    ]
    args = [xs, modv, modv, w]
    if rope:
        in_specs += [pl.BlockSpec((2, tm, LANES), lambda i: (0, i, 0))] * 2
        args += list(rope_tabs)
    out_specs = pl.BlockSpec((tm, n), lambda i: (i, 0))
    out_shape = jax.ShapeDtypeStruct((tp, n), BF16)
    if rope:
        per = tm // KEY_CHUNK
        out_specs = [out_specs, pl.BlockSpec((per, A_HEADS * HEAD_DIM, KEY_CHUNK), lambda i: (i, 0, 0))]
        out_shape = [out_shape, jax.ShapeDtypeStruct((tp // KEY_CHUNK, A_HEADS * HEAD_DIM, KEY_CHUNK), BF16)]
    return pl.pallas_call(
        functools.partial(_proj_kernel, tn=512, rope=rope),
        grid=(tp // tm,),
        in_specs=in_specs,
        out_specs=out_specs,
        out_shape=out_shape,
        compiler_params=_cparams(("arbitrary",)),
        name="mod_projection_rope" if rope else "mod_projection",
    )(*args)


def _attn_a_kernel(lq1_ref, lk1_ref, lq2_ref, lk2_ref, subln_ref, beta_ref, q_ref, k_ref, vt_ref, o_ref,
                   qs_sc, s0_sc, s1_sc, m_sc, acc_sc, *, s_lat, tq, n_sub, n_lat_tiles, lam_init):
    i = pl.program_id(1)
    lane = lax.broadcasted_iota(jnp.int32, (1, LANES), 1)
    map1 = ((lane // 32) % 2) == 1
    q = q_ref[...]
    zero = jnp.zeros_like(q)
    qs_sc[0:tq, :] = jnp.where(map1, zero, q)
    qs_sc[tq:2 * tq, :] = jnp.where(map1, q, zero)
    m_sc[...] = jnp.full_like(m_sc, -jnp.inf)
    acc_sc[...] = jnp.zeros_like(acc_sc)
    ones_rows = jnp.where(lax.broadcasted_iota(jnp.int32, (A_ONES_ROWS, KEY_CHUNK), 0) == 0, 1.0, 0.0).astype(BF16)

    def keys(c):
        start = c * KEY_CHUNK
        if not isinstance(c, int):
            start = pl.multiple_of(start, KEY_CHUNK)
        return k_ref[pl.ds(start, KEY_CHUNK), :]

    def scores(group, s_sc):
        for j in range(n_sub):
            s_sc[j] = _dot_nt(keys(group * n_sub + j), qs_sc[...])

    def update(chunks, s):
        m_prev = m_sc[...]
        m_new = m_prev
        for sj in s:
            m_new = jnp.maximum(m_new, jnp.max(sj, axis=0, keepdims=True))
        acc = jnp.exp2(m_prev - m_new) * acc_sc[...]
        for c, sj in zip(chunks, s):
            p = jnp.exp2(sj - m_new).astype(BF16)
            v1 = jnp.concatenate([vt_ref[c], ones_rows], axis=0)
            acc = acc + jnp.dot(v1, p, preferred_element_type=F32)
        acc_sc[...] = acc
        m_sc[...] = m_new

    ctx_chunk = s_lat // KEY_CHUNK
    update([ctx_chunk], [_dot_nt(keys(ctx_chunk), qs_sc[...])])

    @pl.when(i < n_lat_tiles)
    def _():
        n_groups = s_lat // (KEY_CHUNK * n_sub)
        scores(0, s0_sc)

        def body(g2, carry):
            g = 2 * g2
            scores(g + 1, s1_sc)
            update([g * n_sub + j for j in range(n_sub)], [s0_sc[j] for j in range(n_sub)])
            scores(jnp.minimum(g + 2, n_groups - 1), s0_sc)
            update([(g + 1) * n_sub + j for j in range(n_sub)], [s1_sc[j] for j in range(n_sub)])
            return carry
        lax.fori_loop(0, n_groups // 2, body, 0)

    lam = (jnp.exp(jnp.sum(lq1_ref[...] * lk1_ref[...], axis=-1, keepdims=True))
           - jnp.exp(jnp.sum(lq2_ref[...] * lk2_ref[...], axis=-1, keepdims=True)) + lam_init)
    l = acc_sc[HEAD_DIM:HEAD_DIM + 1, :]
    o_t = (acc_sc[0:HEAD_DIM, 0:tq] / l[:, 0:tq] - lam * (acc_sc[0:HEAD_DIM, tq:2 * tq] / l[:, tq:2 * tq]))
    o = o_t.T
    y = o * lax.rsqrt(jnp.mean(o * o, axis=-1, keepdims=True) + LN_EPS) * subln_ref[...]
    o_ref[...] = (y * (1.0 - lam_init) * beta_ref[...]).astype(o_ref.dtype)


def _attention_a(proj, vt, lq1, lk1, lq2, lk2, subln, beta, s_lat, n_ctx, lam_init):
    tp = proj.shape[0]
    assert n_ctx == KEY_CHUNK
    tq, n_sub = 256, 2
    n_lat_tiles = s_lat // tq
    vec = lambda: pl.BlockSpec((1, A_QK_DIM), lambda h, i: (0, 0))
    return pl.pallas_call(
        functools.partial(_attn_a_kernel, s_lat=s_lat, tq=tq, n_sub=n_sub, n_lat_tiles=n_lat_tiles,
                          lam_init=lam_init),
        grid=(A_HEADS, tp // tq),
        in_specs=[
            vec(), vec(), vec(), vec(),
            pl.BlockSpec((1, HEAD_DIM), lambda h, i: (0, 0)),
            pl.BlockSpec((1, HEAD_DIM), lambda h, i: (0, h)),
            pl.BlockSpec((tq, HEAD_DIM), lambda h, i: (i, COL_AQ + h)),
            pl.BlockSpec((tp, HEAD_DIM), lambda h, i: (0, COL_AK + h)),
            pl.BlockSpec((tp // KEY_CHUNK, HEAD_DIM, KEY_CHUNK), lambda h, i: (0, h, 0)),
        ],
        out_specs=pl.BlockSpec((tq, HEAD_DIM), lambda h, i: (i, h)),
        out_shape=jax.ShapeDtypeStruct((tp, A_HEADS * HEAD_DIM), BF16),
        scratch_shapes=[
            pltpu.VMEM((2 * tq, HEAD_DIM), BF16),
            pltpu.VMEM((n_sub, KEY_CHUNK, 2 * tq), F32),
            pltpu.VMEM((n_sub, KEY_CHUNK, 2 * tq), F32),
            pltpu.VMEM((1, 2 * tq), F32),
            pltpu.VMEM((HEAD_DIM + A_ONES_ROWS, 2 * tq), F32),
        ],
        compiler_params=_cparams(("arbitrary", "arbitrary")),
        name="attn_diff",
    )(lq1, lk1, lq2, lk2, subln, beta, proj, proj, vt)


def _attn_b_kernel(sink_ref, beta_ref, q_ref, kp_ref, kc_ref, kn_ref, kx_ref, vp_ref, vc_ref, vn_ref, vx_ref,
                   o_ref, *, s_lat, tq):
    kv = pl.program_id(0)
    i = pl.program_id(1)
    scale = HEAD_DIM ** -0.5
    q0 = i * tq
    qpos = q0 + lax.broadcasted_iota(jnp.int32, (tq, 1), 0)

    def valid(kstart, n):
        kpos = kstart + lax.broadcasted_iota(jnp.int32, (1, n), 1)
        return (jnp.abs(kpos - qpos) <= WINDOW) & (kpos >= 0) & (kpos < s_lat) & (qpos < s_lat)

    ok_p = valid(q0 - WINDOW, WINDOW)
    ok_c = valid(q0, tq)
    ok_n = valid(q0 + tq, WINDOW)
    neg = -jnp.inf
    for g in range(B_GROUP):
        qg = q_ref[:, g * HEAD_DIM:(g + 1) * HEAD_DIM]
        s_x = _dot_nt(qg, kx_ref[...]) * scale
        s_p = jnp.where(ok_p, _dot_nt(qg, kp_ref[...]) * scale, neg)
        s_c = jnp.where(ok_c, _dot_nt(qg, kc_ref[...]) * scale, neg)
        s_n = jnp.where(ok_n, _dot_nt(qg, kn_ref[...]) * scale, neg)
        snk = sink_ref[kv * B_GROUP + g]
        mx = jnp.maximum(jnp.maximum(jnp.max(s_x, -1, keepdims=True), jnp.max(s_p, -1, keepdims=True)),
                         jnp.maximum(jnp.max(s_c, -1, keepdims=True), jnp.max(s_n, -1, keepdims=True)))
        mx = jnp.maximum(mx, snk)
        e_x = jnp.exp(s_x - mx)
        e_p = jnp.exp(s_p - mx)
        e_c = jnp.exp(s_c - mx)
        e_n = jnp.exp(s_n - mx)
        den = (jnp.exp(snk - mx) + jnp.sum(e_x, -1, keepdims=True) + jnp.sum(e_p, -1, keepdims=True)
               + jnp.sum(e_c, -1, keepdims=True) + jnp.sum(e_n, -1, keepdims=True))
        o = (jnp.dot(e_x.astype(BF16), vx_ref[...], preferred_element_type=F32)
             + jnp.dot(e_p.astype(BF16), vp_ref[...], preferred_element_type=F32)
             + jnp.dot(e_c.astype(BF16), vc_ref[...], preferred_element_type=F32)
             + jnp.dot(e_n.astype(BF16), vn_ref[...], preferred_element_type=F32))
        o_ref[:, g * HEAD_DIM:(g + 1) * HEAD_DIM] = (
            (o / den) * beta_ref[:, g * HEAD_DIM:(g + 1) * HEAD_DIM]).astype(o_ref.dtype)


def _attention_b(proj, sink, beta, s_lat, n_ctx):
    tp = proj.shape[0]
    tq = 256
    per = tq // WINDOW
    last_halo = tp // WINDOW - 1
    gw = B_GROUP * HEAD_DIM

    def halo_prev(col):
        return pl.BlockSpec((WINDOW, HEAD_DIM), lambda kv, i: (jnp.maximum(i * per - 1, 0), col + kv))

    def halo_next(col):
        return pl.BlockSpec((WINDOW, HEAD_DIM), lambda kv, i: (jnp.minimum((i + 1) * per, last_halo), col + kv))

    def cur(col):
        return pl.BlockSpec((tq, HEAD_DIM), lambda kv, i: (i, col + kv))

    def ctx(col):
        return pl.BlockSpec((n_ctx, HEAD_DIM), lambda kv, i: (s_lat // n_ctx, col + kv))

    return pl.pallas_call(
        functools.partial(_attn_b_kernel, s_lat=s_lat, tq=tq),
        grid=(B_KV_HEADS, tp // tq),
        in_specs=[
            pl.BlockSpec(memory_space=pltpu.SMEM),
            pl.BlockSpec((1, gw), lambda kv, i: (0, (A_HEADS * HEAD_DIM) // gw + kv)),
            pl.BlockSpec((tq, gw), lambda kv, i: (i, (COL_BQ * LANES) // gw + kv)),
            halo_prev(COL_BK), cur(COL_BK), halo_next(COL_BK), ctx(COL_BK),
            halo_prev(COL_BV), cur(COL_BV), halo_next(COL_BV), ctx(COL_BV),
        ],
        out_specs=pl.BlockSpec((tq, gw), lambda kv, i: (i, kv)),
        out_shape=jax.ShapeDtypeStruct((tp, B_HEADS * HEAD_DIM), BF16),
        compiler_params=_cparams(("arbitrary", "arbitrary")),
        name="attn_window",
    )(sink, beta, proj, proj, proj, proj, proj, proj, proj, proj, proj)


def _na_bias_tables(rpb, s_lat):
    rows = s_lat // GRID_W
    tr = TOKEN_TILE // GRID_W
    kh = min(NA_KH, rows)

    neg = -1e30
    n_heads = rpb.shape[0]
    per_c = []
    for c in range(GRID_W):
        c0 = min(max(c - NA_KW // 2, 0), GRID_W - NA_KW)
        lo = c0 - c + NA_KW - 1
        per_c.append(jnp.pad(rpb[:, :, lo:lo + NA_KW], ((0, 0), (0, 0), (c0, GRID_W - NA_KW - c0)),
                             constant_values=neg))
    colb = jnp.stack(per_c, axis=2).astype(F32)
    off = jnp.full((n_heads, GRID_W, GRID_W), neg, F32)

    def table(q_row0, n_rows):
        per_r = []
        for r in range(q_row0, q_row0 + tr):
            r0 = min(max(r - kh // 2, 0), n_rows - kh)
            blocks = []
            for kr in range(q_row0 - tr, q_row0 + 2 * tr):
                inside = r0 <= kr < r0 + kh and 0 <= kr < n_rows
                blocks.append(colb[:, kr - r + NA_KH - 1] if inside else off)
            per_r.append(jnp.concatenate(blocks, axis=-1))
        return jnp.stack(per_r, axis=1).reshape(n_heads, TOKEN_TILE, 3 * TOKEN_TILE)

    far = 4 * tr
    out = [table(0, rows), table(far, 2 * far + tr), table(rows - tr, rows)]
    out.append(jnp.full_like(out[0], neg))
    return jnp.stack(out, axis=0)


def _attn_c_kernel(beta_ref, bias_ref, q_ref, kp_ref, kc_ref, kn_ref, kx_ref, vp_ref, vc_ref, vn_ref, vx_ref,
                   o_ref, *, tq):
    scale = HEAD_DIM ** -0.5
    q = q_ref[...]
    s_x = _dot_nt(q, kx_ref[...]) * scale
    s_p = _dot_nt(q, kp_ref[...]) * scale + bias_ref[0, 0, :, 0:tq]
    s_c = _dot_nt(q, kc_ref[...]) * scale + bias_ref[0, 0, :, tq:2 * tq]
    s_n = _dot_nt(q, kn_ref[...]) * scale + bias_ref[0, 0, :, 2 * tq:3 * tq]
    mx = jnp.maximum(jnp.maximum(jnp.max(s_x, -1, keepdims=True), jnp.max(s_p, -1, keepdims=True)),
                     jnp.maximum(jnp.max(s_c, -1, keepdims=True), jnp.max(s_n, -1, keepdims=True)))
    e_x = jnp.exp(s_x - mx)
    e_p = jnp.exp(s_p - mx)
    e_c = jnp.exp(s_c - mx)
    e_n = jnp.exp(s_n - mx)
    den = (jnp.sum(e_x, -1, keepdims=True) + jnp.sum(e_p, -1, keepdims=True)
           + jnp.sum(e_c, -1, keepdims=True) + jnp.sum(e_n, -1, keepdims=True))
    o = (jnp.dot(e_x.astype(BF16), vx_ref[...], preferred_element_type=F32)
         + jnp.dot(e_p.astype(BF16), vp_ref[...], preferred_element_type=F32)
         + jnp.dot(e_c.astype(BF16), vc_ref[...], preferred_element_type=F32)
         + jnp.dot(e_n.astype(BF16), vn_ref[...], preferred_element_type=F32))
    o_ref[...] = ((o / den) * beta_ref[...]).astype(o_ref.dtype)


def _attention_c(proj, bias_tabs, beta, s_lat, n_ctx):
    tp = proj.shape[0]
    tq = TOKEN_TILE
    n_lat_tiles = s_lat // tq
    n_tiles = tp // tq
    beta_col0 = (A_HEADS + B_HEADS)

    def kind(i):
        return jnp.where(i >= n_lat_tiles, 3, jnp.where(i == 0, 0, jnp.where(i == n_lat_tiles - 1, 2, 1)))

    def prev(col):
        return pl.BlockSpec((tq, HEAD_DIM), lambda h, i: (jnp.maximum(i - 1, 0), col + h))

    def cur(col):
        return pl.BlockSpec((tq, HEAD_DIM), lambda h, i: (i, col + h))

    def nxt(col):
        return pl.BlockSpec((tq, HEAD_DIM), lambda h, i: (jnp.minimum(i + 1, n_tiles - 1), col + h))

    def ctx(col):
        return pl.BlockSpec((n_ctx, HEAD_DIM), lambda h, i: (s_lat // n_ctx, col + h))

    return pl.pallas_call(
        functools.partial(_attn_c_kernel, tq=tq),
        grid=(C_HEADS, n_tiles),
        in_specs=[
            pl.BlockSpec((1, HEAD_DIM), lambda h, i: (0, beta_col0 + h)),
            pl.BlockSpec((1, 1, tq, 3 * tq), lambda h, i: (kind(i), h, 0, 0)),
            cur(COL_CQ),
            prev(COL_CK), cur(COL_CK), nxt(COL_CK), ctx(COL_CK),
            prev(COL_CV), cur(COL_CV), nxt(COL_CV), ctx(COL_CV),
        ],
        out_specs=pl.BlockSpec((tq, HEAD_DIM), lambda h, i: (i, h)),
        out_shape=jax.ShapeDtypeStruct((tp, C_HEADS * HEAD_DIM), BF16),
        compiler_params=_cparams(("arbitrary", "arbitrary")),
        name="attn_neighbourhood",
    )(beta, bias_tabs, proj, proj, proj, proj, proj, proj, proj, proj, proj)


def _outproj_kernel(ma_ref, mb_ref, mc_ref, w_ref, x_ref, gate_ref, g_ref, b_ref, o_ref, *, alpha):
    wa = A_HEADS * HEAD_DIM
    wb = wa + B_HEADS * HEAD_DIM
    y = (jnp.dot(ma_ref[...], w_ref[0:wa, :], preferred_element_type=F32)
         + jnp.dot(mb_ref[...], w_ref[wa:wb, :], preferred_element_type=F32)
         + jnp.dot(mc_ref[...], w_ref[wb:, :], preferred_element_type=F32))
    z = alpha * x_ref[...] + gate_ref[0] * y
    o_ref[...] = _layer_norm(z, g_ref[...], b_ref[...])


def _out_projection(ma, mb, mc, w_out, xs, modv, layer, n_lat_tiles, ln_g, ln_b, alpha):
    tp, d = xs.shape
    tm = TOKEN_TILE // 2
    n_lat = n_lat_tiles * (TOKEN_TILE // tm)
    row = lambda i: (i, 0)
    fixed = lambda i: (0, 0)
    return pl.pallas_call(
        functools.partial(_outproj_kernel, alpha=alpha),
        grid=(tp // tm,),
        in_specs=[
            pl.BlockSpec((tm, ma.shape[1]), row),
            pl.BlockSpec((tm, mb.shape[1]), row),
            pl.BlockSpec((tm, mc.shape[1]), row),
            pl.BlockSpec(w_out.shape, fixed, pipeline_mode=pl.Buffered(1)),
            pl.BlockSpec((tm, d), row),
            pl.BlockSpec((1, 1, d), _mod_index(layer, 2, n_lat)),
            pl.BlockSpec((1, d), fixed),
            pl.BlockSpec((1, d), fixed),
        ],
        out_specs=pl.BlockSpec((tm, d), row),
        out_shape=jax.ShapeDtypeStruct((tp, d), F32),
        compiler_params=_cparams(("arbitrary",)),
        name="out_projection_ln",
    )(ma, mb, mc, w_out, xs, modv, ln_g, ln_b)


def _take_top(vals, rounds, want_rank):
    tops, cnts = [], []
    n, t = vals.shape
    rank = jnp.full((n, t), float(n), F32) if want_rank else None
    seen = jnp.zeros((1, t), F32)
    for _ in range(rounds):
        m = jnp.max(vals, axis=0, keepdims=True)
        eq = vals == m
        cnt = jnp.sum(jnp.where(eq, 1.0, 0.0), axis=0, keepdims=True)
        if want_rank:
            rank = jnp.minimum(rank, jnp.where(eq, seen, float(n)))
            seen = seen + cnt
        tops.append(m)
        cnts.append(cnt)
        vals = jnp.where(eq, -jnp.inf, vals)
    return jnp.concatenate(tops, axis=0), jnp.concatenate(cnts, axis=0), rank


def _pair_table(a, b, op):
    half = PEER_TOPK // 2
    pieces = [op(a[0:1], b)] + [op(a[r:r + 1], b[0:half]) for r in range(1, half)] + [op(a[half:], b[0:1])]
    return jnp.concatenate(pieces, axis=0)


def _route_kernel(q_ref, keys_ref, n1_ref, rk2_ref, p1_ref, p2_ref, s_sc):
    k = PEER_TOPK
    tt = q_ref.shape[0]
    for h in range(PEER_HEADS):
        s_sc[0] = _dot_nt(keys_ref[h, 0], q_ref[:, (2 * h) * LANES:(2 * h + 1) * LANES])
        s_sc[1] = _dot_nt(keys_ref[h, 1], q_ref[:, (2 * h + 1) * LANES:(2 * h + 2) * LANES])
        for lc in range(tt // LANES):
            cols = slice(lc * LANES, (lc + 1) * LANES)
            s1 = s_sc[0, :, cols]
            s2 = s_sc[1, :, cols]
            v1, c1, _ = _take_top(s1, k, False)
            v2, c2, rk2 = _take_top(s2, k, True)
            cand = _pair_table(v1, v2, jnp.add)
            mult = _pair_table(c1, c2, jnp.multiply)
            thr = jnp.full((1, LANES), jnp.inf, F32)
            seen = jnp.zeros((1, LANES), F32)
            rem = cand
            for _ in range(k):
                m = jnp.max(rem, axis=0, keepdims=True)
                eq = rem == m
                thr = jnp.where(seen < k, m, thr)
                seen = seen + jnp.sum(jnp.where(eq, mult, 0.0), axis=0, keepdims=True)
                rem = jnp.where(eq, -jnp.inf, rem)
            pair = _pair_table(jnp.exp(v1 - v1[0:1]), jnp.exp(v2 - v2[0:1]), jnp.multiply)
            z = jnp.sum(jnp.where(cand >= thr, mult * pair, 0.0), axis=0, keepdims=True)
            n1 = jnp.zeros_like(s1)
            for r in range(k):
                n1 = n1 + jnp.where(s1 + v2[r:r + 1] >= thr, c2[r:r + 1], 0.0)
            n1_ref[h, :, cols] = n1
            rk2_ref[h, :, cols] = rk2.astype(BF16)
            p1_ref[h, :, cols] = jnp.exp(s1 - v1[0:1])
            p2_ref[h, :, cols] = (jnp.exp(s2 - v2[0:1]) / z).astype(BF16)


def _peer_route(qp, keys):
    tp = qp.shape[0]
    tt = TOKEN_TILE
    big = pl.BlockSpec((PEER_HEADS, PEER_KEYS, tt), lambda j: (0, 0, j))
    wide = jax.ShapeDtypeStruct((PEER_HEADS, PEER_KEYS, tp), F32)
    narrow = jax.ShapeDtypeStruct((PEER_HEADS, PEER_KEYS, tp), BF16)
    return pl.pallas_call(
        _route_kernel,
        grid=(tp // tt,),
        in_specs=[
            pl.BlockSpec((tt, qp.shape[1]), lambda j: (j, 0)),
            pl.BlockSpec(keys.shape, lambda j: (0, 0, 0, 0)),
        ],
        out_specs=[big, big, big, big],
        out_shape=[wide, narrow, wide, narrow],
        scratch_shapes=[pltpu.VMEM((2, PEER_KEYS, tt), F32)],
        compiler_params=_cparams(("arbitrary",)),
        name="peer_route",
    )(qp, keys)


def _gelu(a):
    return 0.5 * a * (1.0 + lax.erf(a * (2.0 ** -0.5)))


def _peer_expert_kernel(x_ref, sc_ref, sh_ref, gate_ref, g_ref, b_ref, u_ref, v_ref,
                        n1_ref, rk2_ref, p1_ref, p2_ref, o_ref, ux_sc, w0_sc, w1_sc, at_sc, acc_sc,
                        *, alpha, te, tt, n_e):
    i = pl.program_id(1)
    pack = BF16_SUBLANES
    d = acc_sc.shape[1]
    n_blk = te // PEER_KEYS

    @pl.when(i == 0)
    def _():
        ux_sc[...] = (x_ref[...] * (1.0 + sc_ref[0]) + sh_ref[0]).astype(BF16)
        acc_sc[...] = jnp.zeros_like(acc_sc)
        w1_sc[...] = jnp.zeros_like(w1_sc)

    zero = jnp.zeros((pack, LANES), BF16)

    eg = at_sc.shape[1] // PEER_KEYS
    tg = at_sc.shape[2] // LANES
    n_eg, n_tg = n_blk // eg, tt // LANES // tg
    n_pieces = n_eg * n_tg
    dn = d // n_pieces

    def hidden(r):
        ge, gt = divmod(r, n_tg)
        at_sc[r % 2] = _dot_nt(u_ref[ge * eg * PEER_KEYS:(ge + 1) * eg * PEER_KEYS, :],
                               ux_sc[gt * tg * LANES:(gt + 1) * tg * LANES, :])

    def cell(cur_sc, r, ca, cl):
        ge, gt = divmod(r, n_tg)
        a, lc = ge * eg + ca, gt * tg + cl
        cols = slice(lc * LANES, (lc + 1) * LANES)
        a_t = at_sc.at[r % 2, ca * PEER_KEYS:(ca + 1) * PEER_KEYS, cl * LANES:(cl + 1) * LANES]
        n1 = [jnp.broadcast_to(n1_ref[h, a:a + 1, cols], (pack, LANES)).astype(BF16) for h in range(PEER_HEADS)]
        p1 = [jnp.broadcast_to(p1_ref[h, a:a + 1, cols], (pack, LANES)).astype(BF16) for h in range(PEER_HEADS)]
        pieces = []
        for rg in range(PEER_KEYS // pack):
            rows = slice(rg * pack, (rg + 1) * pack)
            g = zero
            for h in range(PEER_HEADS):
                sel_p1 = jnp.minimum(jnp.maximum(n1[h] - rk2_ref[h, rows, cols], zero), p1[h])
                g = g + sel_p1 * p2_ref[h, rows, cols]
            pieces.append(g * _gelu(a_t[rows, :]).astype(BF16))
        blk = jnp.concatenate(pieces, axis=0)
        cur_sc[cols, a * PEER_KEYS:(a + 1) * PEER_KEYS] = blk.T

    def step(cur_sc, prev_sc):
        hidden(0)
        for r in range(n_pieces):
            cells = [(ca, cl) for ca in range(eg) for cl in range(tg)]
            half = len(cells) // 2
            if r + 1 < n_pieces:
                hidden(r + 1)
            for ca, cl in cells[:half]:
                cell(cur_sc, r, ca, cl)
            acc_sc[:, r * dn:(r + 1) * dn] += jnp.dot(prev_sc[...], v_ref[:, r * dn:(r + 1) * dn],
                                                      preferred_element_type=F32)
            for ca, cl in cells[half:]:
                cell(cur_sc, r, ca, cl)

    @pl.when((i < n_e) & (i % 2 == 0))
    def _():
        step(w0_sc, w1_sc)

    @pl.when((i < n_e) & (i % 2 == 1))
    def _():
        step(w1_sc, w0_sc)

    @pl.when(i == n_e)
    def _():
        last_sc = w1_sc if n_e % 2 == 0 else w0_sc
        y = acc_sc[...] + jnp.dot(last_sc[...], v_ref[...], preferred_element_type=F32)
        z = alpha * x_ref[...] + gate_ref[0] * y
        o_ref[...] = _layer_norm(z, g_ref[...], b_ref[...])


def _peer_experts(xs, modv, layer, n_lat_tiles, n_out_tiles, ln_g, ln_b, u, v, route, alpha):
    tp, d = xs.shape
    tt = TOKEN_TILE
    te = 8 * PEER_KEYS
    n_exp = u.shape[0]
    n1, rk2, p1, p2 = route
    once = pl.Buffered(1)
    tok = lambda j, i: (j, 0)
    fixed = lambda j, i: (0, 0)
    mod = lambda chunk: (lambda j, i: _mod_index(layer, chunk, n_lat_tiles)(j))
    rt = lambda: pl.BlockSpec((PEER_HEADS, PEER_KEYS, tt), lambda j, i: (0, 0, j), pipeline_mode=once)
    n_e = n_exp // te
    build = lambda i: jnp.minimum(i, n_e - 1)
    drain = lambda i: jnp.maximum(i - 1, 0)
    rt1 = lambda: pl.BlockSpec((PEER_HEADS, te // PEER_KEYS, tt), lambda j, i: (0, build(i), j))
    return pl.pallas_call(
        functools.partial(_peer_expert_kernel, alpha=alpha, te=te, tt=tt, n_e=n_e),
        grid=(n_out_tiles, n_e + 1),
        in_specs=[
            pl.BlockSpec((tt, d), tok, pipeline_mode=once),
            pl.BlockSpec((1, 1, d), mod(4)),
            pl.BlockSpec((1, 1, d), mod(3)),
            pl.BlockSpec((1, 1, d), mod(5)),
            pl.BlockSpec((1, d), fixed),
            pl.BlockSpec((1, d), fixed),
            pl.BlockSpec((te, d), lambda j, i: (build(i), 0)),
            pl.BlockSpec((te, d), lambda j, i: (drain(i), 0)),
            rt1(), rt(), rt1(), rt(),
        ],
        out_specs=pl.BlockSpec((tt, d), tok),
        out_shape=jax.ShapeDtypeStruct((n_out_tiles * tt, d), F32),
        scratch_shapes=[
            pltpu.VMEM((tt, d), BF16),
            pltpu.VMEM((tt, te), BF16),
            pltpu.VMEM((tt, te), BF16),
            pltpu.VMEM((2, 2 * PEER_KEYS, 2 * LANES), F32),
            pltpu.VMEM((tt, d), F32),
        ],
        compiler_params=_cparams(("arbitrary", "arbitrary")),
        name="peer_experts_ln",
    )(xs, modv, modv, modv, ln_g, ln_b, u, v, n1, rk2, p1, p2)


def _permute_qk_columns(w):
    d = w.shape[0]
    a_end, b_start, b_end = COL_AV * LANES, COL_BQ * LANES, COL_BV * LANES
    wa = w[:, :a_end].reshape(d, COL_AV, 2, A_QK_DIM // 2, 2)
    wa = wa.transpose(0, 1, 4, 2, 3).reshape(d, a_end)
    wb = w[:, b_start:b_end].reshape(d, COL_BV - COL_BQ, HEAD_DIM // 2, 2)
    wb = wb.transpose(0, 1, 3, 2).reshape(d, b_end - b_start)
    return jnp.concatenate([wa, w[:, a_end:b_start], wb, w[:, b_end:]], axis=1)


def _rope_tables(s_lat, tp):
    t = jnp.arange(s_lat)
    row = (t // GRID_W).astype(F32)
    col = (t % GRID_W).astype(F32)
    lane = np.arange(LANES)
    sign = jnp.asarray(np.where(lane < LANES // 2, -1.0, 1.0), F32)

    def tabs(dim, lane_to_pair):
        d_axis = dim // 2
        inv = ROPE_BASE ** (-jnp.arange(0, d_axis, 2, dtype=F32) / d_axis)
        ang = jnp.concatenate([row[:, None] * inv, col[:, None] * inv], axis=-1)
        ang = ang[:, lane_to_pair]
        pad = ((0, tp - s_lat), (0, 0))
        return (jnp.pad(jnp.cos(ang), pad, constant_values=1.0),
                jnp.pad(jnp.sin(ang) * sign, pad, constant_values=0.0))

    ca, sa = tabs(A_QK_DIM, lane % 32)
    cb, sb = tabs(HEAD_DIM, lane % 64)
    return jnp.stack([ca, cb]), jnp.stack([sa, sb])


def kernel(x, c, ctx, c_ctx, w_ada, b_ada, w_in, w_out, beta_out, ln1_g, ln1_b, ln2_g, ln2_b, diff_lq1, diff_lk1, diff_lq2, diff_lk2, diff_subln, sink, na_rpb, peer_wq, peer_keys, peer_u, peer_v):
    depth = w_ada.shape[0]
    _, s_lat, d = x.shape
    n_ctx = ctx.shape[1]
    assert x.shape[0] == 1 and s_lat % TOKEN_TILE == 0 and n_ctx == 256 and s_lat % n_ctx == 0
    n_lat_tiles = s_lat // TOKEN_TILE
    tp = (n_lat_tiles + 1) * TOKEN_TILE
    alpha = (2 * depth) ** 0.25

    xs = jnp.concatenate([x[0], ctx[0], jnp.zeros((tp - s_lat - n_ctx, d), x.dtype)], axis=0)
    cc = jnp.concatenate([c[0:1], c_ctx[None, :], jnp.zeros((6, d), c.dtype)], axis=0)
    mod = _ada_modulation(cc, w_ada, b_ada)
    modv = mod[:, :2, :].reshape(depth * 2 * 6, 1, d)
    rope_tabs = _rope_tables(s_lat, tp)

    for l in range(depth):
        lam_init = 0.8 - 0.6 * math.exp(-0.3 * l)
        last = l == depth - 1
        w_in_l = _permute_qk_columns(w_in[l].astype(BF16))
        proj, vt = _projection(xs, modv, l, 1, 0, w_in_l, n_lat_tiles, rope_tabs)
        beta = beta_out[l][None, :]
        ma = _attention_a(proj, vt, diff_lq1[l][None], diff_lk1[l][None], diff_lq2[l][None], diff_lk2[l][None],
                          diff_subln[l][None], beta, s_lat, n_ctx, lam_init)
        mb = _attention_b(proj, sink[l], beta, s_lat, n_ctx)
        mc = _attention_c(proj, _na_bias_tables(na_rpb[l], s_lat), beta, s_lat, n_ctx)
        xs = _out_projection(ma, mb, mc, w_out[l].astype(BF16), xs, modv, l, n_lat_tiles,
                             ln1_g[l][None], ln1_b[l][None], alpha)
        qp = _projection(xs, modv, l, 4, 3, peer_wq[l].astype(BF16), n_lat_tiles)
        route = _peer_route(qp, peer_keys[l].astype(BF16))
        xs = _peer_experts(xs, modv, l, n_lat_tiles, n_lat_tiles if last else n_lat_tiles + 1,
                           ln2_g[l][None], ln2_b[l][None], peer_u[l].astype(BF16), peer_v[l].astype(BF16),
                           route, alpha)
    return xs[None]
```

```python
import functools
import math

import numpy as np
import jax
import jax.numpy as jnp
from jax import lax
from jax.experimental import pallas as pl
from jax.experimental.pallas import tpu as pltpu

F32 = jnp.float32
BF16 = jnp.bfloat16

GRID_W = 64
HEAD_DIM = 128
A_HEADS = 4
B_HEADS = 8
B_KV_HEADS = 2
B_GROUP = 4
C_HEADS = 4
A_QK_DIM = 64
WINDOW = 128
NA_KH = 8
NA_KW = 16
PEER_HEADS = 8
PEER_KEYS = 128
PEER_TOPK = 16
ROPE_BASE = 10000.0
LN_EPS = 1e-5

LANES = 128
BF16_SUBLANES = 16
TOKEN_TILE = 512
KEY_CHUNK = 256
A_ONES_ROWS = 16
A_Q_SCALE = A_QK_DIM ** -0.5 * math.log2(math.e)
VMEM_LIMIT = 56 * 1024 * 1024

COL_AQ, COL_AK, COL_AV = 0, 4, 8
COL_BQ, COL_BK, COL_BV = 12, 20, 22
COL_CQ, COL_CK, COL_CV = 24, 28, 32
N_COLBLOCKS = 36


def _cparams(sem):
    return pltpu.CompilerParams(dimension_semantics=sem, vmem_limit_bytes=VMEM_LIMIT)


def _dot_nt(a, b):
    return lax.dot_general(a, b, (((1,), (1,)), ((), ())), preferred_element_type=F32)


def _dot_tn(a, b):
    return lax.dot_general(a, b, (((0,), (0,)), ((), ())), preferred_element_type=F32)


def _layer_norm(y, g, b):
    mu = jnp.mean(y, axis=-1, keepdims=True)
    d = y - mu
    var = jnp.mean(d * d, axis=-1, keepdims=True)
    return d * lax.rsqrt(var + LN_EPS) * g + b


def _ada_kernel(c_ref, w_ref, b_ref, o_ref):
    c = c_ref[...]
    a = c * jax.nn.sigmoid(c)
    o_ref[0] = jnp.dot(a.astype(BF16), w_ref[0].astype(BF16), preferred_element_type=F32) + b_ref[0]


def _ada_modulation(cc, w_ada, b_ada):
    depth, d, n = w_ada.shape
    tn = 1024
    return pl.pallas_call(
        _ada_kernel,
        grid=(depth, n // tn),
        in_specs=[
            pl.BlockSpec((8, d), lambda l, j: (0, 0)),
            pl.BlockSpec((1, d, tn), lambda l, j: (l, 0, j)),
            pl.BlockSpec((1, 1, tn), lambda l, j: (l, 0, j)),
        ],
        out_specs=pl.BlockSpec((1, 8, tn), lambda l, j: (l, 0, j)),
        out_shape=jax.ShapeDtypeStruct((depth, 8, n), F32),
        compiler_params=_cparams(("arbitrary", "arbitrary")),
        name="ada_modulation",
    )(cc, w_ada, b_ada.reshape(depth, 1, n))


def _rope_kind(colblock):
    if colblock < COL_AV:
        return 0
    if COL_BQ <= colblock < COL_BV:
        return 1
    return None


def _proj_kernel(x_ref, sc_ref, sh_ref, w_ref, *rest, tn, rope):
    if rope:
        cos_ref, sin_ref, o_ref, vt_ref = rest
    else:
        (o_ref,) = rest
    xm = (x_ref[...] * (1.0 + sc_ref[0]) + sh_ref[0]).astype(BF16)
    n = w_ref.shape[1]
    for jn in range(n // tn):
        y = jnp.dot(xm, w_ref[:, jn * tn:(jn + 1) * tn], preferred_element_type=F32)
        for jb in range(tn // LANES):
            colblock = jn * (tn // LANES) + jb
            yb = y[:, jb * LANES:(jb + 1) * LANES]
            kind = _rope_kind(colblock) if rope else None
            if kind is not None:
                yb = yb * cos_ref[kind] + pltpu.roll(yb, LANES // 2, 1) * sin_ref[kind]
            if rope and colblock < COL_AK:
                yb = yb * A_Q_SCALE
            o_ref[:, colblock * LANES:(colblock + 1) * LANES] = yb.astype(o_ref.dtype)
            if rope and COL_AV <= colblock < COL_BQ:
                hd = colblock - COL_AV
                for ck in range(yb.shape[0] // KEY_CHUNK):
                    vt_ref[ck, hd * HEAD_DIM:(hd + 1) * HEAD_DIM, :] = (
                        yb[ck * KEY_CHUNK:(ck + 1) * KEY_CHUNK, :].T.astype(vt_ref.dtype))


def _mod_index(layer_slot, chunk, n_lat_tiles):
    def index(i):
        who = jnp.where(i >= n_lat_tiles, 1, 0)
        return ((layer_slot * 2 + who) * 6 + chunk, 0, 0)
    return index


def _projection(xs, modv, layer, chunk_scale, chunk_shift, w, n_lat_tiles, rope_tabs=None):
    tp, d = xs.shape
    n = w.shape[1]
    tm = TOKEN_TILE
    rope = rope_tabs is not None
    in_specs = [
        pl.BlockSpec((tm, d), lambda i: (i, 0)),
        pl.BlockSpec((1, 1, d), _mod_index(layer, chunk_scale, n_lat_tiles)),
        pl.BlockSpec((1, 1, d), _mod_index(layer, chunk_shift, n_lat_tiles)),
        pl.BlockSpec((d, n), lambda i: (0, 0), pipeline_mode=pl.Buffered(1)),
    ]
    args = [xs, modv, modv, w]
    if rope:
        in_specs += [pl.BlockSpec((2, tm, LANES), lambda i: (0, i, 0))] * 2
        args += list(rope_tabs)
    out_specs = pl.BlockSpec((tm, n), lambda i: (i, 0))
    out_shape = jax.ShapeDtypeStruct((tp, n), BF16)
    if rope:
        per = tm // KEY_CHUNK
        out_specs = [out_specs, pl.BlockSpec((per, A_HEADS * HEAD_DIM, KEY_CHUNK), lambda i: (i, 0, 0))]
        out_shape = [out_shape, jax.ShapeDtypeStruct((tp // KEY_CHUNK, A_HEADS * HEAD_DIM, KEY_CHUNK), BF16)]
    return pl.pallas_call(
        functools.partial(_proj_kernel, tn=512, rope=rope),
        grid=(tp // tm,),
        in_specs=in_specs,
        out_specs=out_specs,
        out_shape=out_shape,
        compiler_params=_cparams(("arbitrary",)),
        name="mod_projection_rope" if rope else "mod_projection",
    )(*args)


def _attn_a_kernel(lq1_ref, lk1_ref, lq2_ref, lk2_ref, subln_ref, beta_ref, q_ref, k_ref, vt_ref, o_ref,
                   qs_sc, s0_sc, s1_sc, p0_sc, p1_sc, m_sc, acc_sc, *, s_lat, tq, n_sub, n_lat_tiles, lam_init):
    i = pl.program_id(1)
    lane = lax.broadcasted_iota(jnp.int32, (1, LANES), 1)
    map1 = ((lane // 32) % 2) == 1
    q = q_ref[...]
    zero = jnp.zeros_like(q)
    qs_sc[0:tq, :] = jnp.where(map1, zero, q)
    qs_sc[tq:2 * tq, :] = jnp.where(map1, q, zero)
    m_sc[...] = jnp.full_like(m_sc, -jnp.inf)
    acc_sc[...] = jnp.zeros_like(acc_sc)
    ones_rows = jnp.where(lax.broadcasted_iota(jnp.int32, (A_ONES_ROWS, KEY_CHUNK), 0) == 0, 1.0, 0.0).astype(BF16)

    def keys(c):
        start = c * KEY_CHUNK
        if not isinstance(c, int):
            start = pl.multiple_of(start, KEY_CHUNK)
        return k_ref[pl.ds(start, KEY_CHUNK), :]

    def scores(group, s_sc):
        for j in range(n_sub):
            s_sc[j] = _dot_nt(keys(group * n_sub + j), qs_sc[...])

    def update(chunks, s):
        m_prev = m_sc[...]
        m_new = m_prev
        for sj in s:
            m_new = jnp.maximum(m_new, jnp.max(sj, axis=0, keepdims=True))
        acc = jnp.exp2(m_prev - m_new) * acc_sc[...]
        for c, sj in zip(chunks, s):
            p = jnp.exp2(sj - m_new).astype(BF16)
            v1 = jnp.concatenate([vt_ref[c], ones_rows], axis=0)
            acc = acc + jnp.dot(v1, p, preferred_element_type=F32)
        acc_sc[...] = acc
        m_sc[...] = m_new

    ctx_chunk = s_lat // KEY_CHUNK
    update([ctx_chunk], [_dot_nt(keys(ctx_chunk), qs_sc[...])])

    def values(group, p_sc):
        out = None
        for j in range(n_sub):
            v1 = jnp.concatenate([vt_ref[group * n_sub + j], ones_rows], axis=0)
            part = jnp.dot(v1, p_sc[j], preferred_element_type=F32)
            out = part if out is None else out + part
        return out

    def stage(g_scores, s_next_sc, g_values, p_prev_sc, s_cur_sc, p_cur_sc):
        acc_sc[...] += values(g_values, p_prev_sc)
        scores(g_scores, s_next_sc)
        m_prev = m_sc[...]
        m_new = m_prev
        for j in range(n_sub):
            m_new = jnp.maximum(m_new, jnp.max(s_cur_sc[j], axis=0, keepdims=True))
        for j in range(n_sub):
            p_cur_sc[j] = jnp.exp2(s_cur_sc[j] - m_new).astype(BF16)
        acc_sc[...] *= jnp.exp2(m_prev - m_new)
        m_sc[...] = m_new

    @pl.when(i < n_lat_tiles)
    def _():
        n_groups = s_lat // (KEY_CHUNK * n_sub)
        scores(0, s0_sc)
        p1_sc[...] = jnp.zeros_like(p1_sc)

        unroll = math.gcd(n_groups, 8)
        assert unroll % 2 == 0

        def body(gu, carry):
            for u in range(unroll):
                g = unroll * gu + u
                bufs = (s1_sc, p1_sc, s0_sc, p0_sc) if u % 2 == 0 else (s0_sc, p0_sc, s1_sc, p1_sc)
                s_next, p_prev, s_cur, p_cur = bufs
                stage(jnp.minimum(g + 1, n_groups - 1), s_next, jnp.maximum(g - 1, 0), p_prev, s_cur, p_cur)
            return carry
        lax.fori_loop(0, n_groups // unroll, body, 0)
        acc_sc[...] += values(n_groups - 1, p1_sc)

    lam = (jnp.exp(jnp.sum(lq1_ref[...] * lk1_ref[...], axis=-1, keepdims=True))
           - jnp.exp(jnp.sum(lq2_ref[...] * lk2_ref[...], axis=-1, keepdims=True)) + lam_init)
    l = acc_sc[HEAD_DIM:HEAD_DIM + 1, :]
    o_t = (acc_sc[0:HEAD_DIM, 0:tq] / l[:, 0:tq] - lam * (acc_sc[0:HEAD_DIM, tq:2 * tq] / l[:, tq:2 * tq]))
    o = o_t.T
    y = o * lax.rsqrt(jnp.mean(o * o, axis=-1, keepdims=True) + LN_EPS) * subln_ref[...]
    o_ref[...] = (y * (1.0 - lam_init) * beta_ref[...]).astype(o_ref.dtype)


def _attention_a(proj, vt, lq1, lk1, lq2, lk2, subln, beta, s_lat, n_ctx, lam_init):
    tp = proj.shape[0]
    assert n_ctx == KEY_CHUNK
    tq, n_sub = 256, 2
    n_lat_tiles = s_lat // tq
    vec = lambda: pl.BlockSpec((1, A_QK_DIM), lambda h, i: (0, 0))
    return pl.pallas_call(
        functools.partial(_attn_a_kernel, s_lat=s_lat, tq=tq, n_sub=n_sub, n_lat_tiles=n_lat_tiles,
                          lam_init=lam_init),
        grid=(A_HEADS, tp // tq),
        in_specs=[
            vec(), vec(), vec(), vec(),
            pl.BlockSpec((1, HEAD_DIM), lambda h, i: (0, 0)),
            pl.BlockSpec((1, HEAD_DIM), lambda h, i: (0, h)),
            pl.BlockSpec((tq, HEAD_DIM), lambda h, i: (i, COL_AQ + h)),
            pl.BlockSpec((tp, HEAD_DIM), lambda h, i: (0, COL_AK + h)),
            pl.BlockSpec((tp // KEY_CHUNK, HEAD_DIM, KEY_CHUNK), lambda h, i: (0, h, 0)),
        ],
        out_specs=pl.BlockSpec((tq, HEAD_DIM), lambda h, i: (i, h)),
        out_shape=jax.ShapeDtypeStruct((tp, A_HEADS * HEAD_DIM), BF16),
        scratch_shapes=[
            pltpu.VMEM((2 * tq, HEAD_DIM), BF16),
            pltpu.VMEM((n_sub, KEY_CHUNK, 2 * tq), F32),
            pltpu.VMEM((n_sub, KEY_CHUNK, 2 * tq), F32),
            pltpu.VMEM((n_sub, KEY_CHUNK, 2 * tq), BF16),
            pltpu.VMEM((n_sub, KEY_CHUNK, 2 * tq), BF16),
            pltpu.VMEM((1, 2 * tq), F32),
            pltpu.VMEM((HEAD_DIM + A_ONES_ROWS, 2 * tq), F32),
        ],
        compiler_params=_cparams(("arbitrary", "arbitrary")),
        name="attn_diff",
    )(lq1, lk1, lq2, lk2, subln, beta, proj, proj, vt)


def _attn_b_kernel(sink_ref, beta_ref, q_ref, kp_ref, kc_ref, kn_ref, kx_ref, vp_ref, vc_ref, vn_ref, vx_ref,
                   o_ref, *, s_lat, tq):
    kv = pl.program_id(0)
    i = pl.program_id(1)
    scale = HEAD_DIM ** -0.5
    q0 = i * tq
    qpos = q0 + lax.broadcasted_iota(jnp.int32, (tq, 1), 0)

    def valid(kstart, n):
        kpos = kstart + lax.broadcasted_iota(jnp.int32, (1, n), 1)
        return (jnp.abs(kpos - qpos) <= WINDOW) & (kpos >= 0) & (kpos < s_lat) & (qpos < s_lat)

    ok_p = valid(q0 - WINDOW, WINDOW)
    ok_c = valid(q0, tq)
    ok_n = valid(q0 + tq, WINDOW)
    neg = -jnp.inf
    for g in range(B_GROUP):
        qg = q_ref[:, g * HEAD_DIM:(g + 1) * HEAD_DIM]
        s_x = _dot_nt(qg, kx_ref[...]) * scale
        s_p = jnp.where(ok_p, _dot_nt(qg, kp_ref[...]) * scale, neg)
        s_c = jnp.where(ok_c, _dot_nt(qg, kc_ref[...]) * scale, neg)
        s_n = jnp.where(ok_n, _dot_nt(qg, kn_ref[...]) * scale, neg)
        snk = sink_ref[kv * B_GROUP + g]
        mx = jnp.maximum(jnp.maximum(jnp.max(s_x, -1, keepdims=True), jnp.max(s_p, -1, keepdims=True)),
                         jnp.maximum(jnp.max(s_c, -1, keepdims=True), jnp.max(s_n, -1, keepdims=True)))
        mx = jnp.maximum(mx, snk)
        e_x = jnp.exp(s_x - mx)
        e_p = jnp.exp(s_p - mx)
        e_c = jnp.exp(s_c - mx)
        e_n = jnp.exp(s_n - mx)
        den = (jnp.exp(snk - mx) + jnp.sum(e_x, -1, keepdims=True) + jnp.sum(e_p, -1, keepdims=True)
               + jnp.sum(e_c, -1, keepdims=True) + jnp.sum(e_n, -1, keepdims=True))
        o = (jnp.dot(e_x.astype(BF16), vx_ref[...], preferred_element_type=F32)
             + jnp.dot(e_p.astype(BF16), vp_ref[...], preferred_element_type=F32)
             + jnp.dot(e_c.astype(BF16), vc_ref[...], preferred_element_type=F32)
             + jnp.dot(e_n.astype(BF16), vn_ref[...], preferred_element_type=F32))
        o_ref[:, g * HEAD_DIM:(g + 1) * HEAD_DIM] = (
            (o / den) * beta_ref[:, g * HEAD_DIM:(g + 1) * HEAD_DIM]).astype(o_ref.dtype)


def _attention_b(proj, sink, beta, s_lat, n_ctx):
    tp = proj.shape[0]
    tq = 256
    per = tq // WINDOW
    last_halo = tp // WINDOW - 1
    gw = B_GROUP * HEAD_DIM

    def halo_prev(col):
        return pl.BlockSpec((WINDOW, HEAD_DIM), lambda kv, i: (jnp.maximum(i * per - 1, 0), col + kv))

    def halo_next(col):
        return pl.BlockSpec((WINDOW, HEAD_DIM), lambda kv, i: (jnp.minimum((i + 1) * per, last_halo), col + kv))

    def cur(col):
        return pl.BlockSpec((tq, HEAD_DIM), lambda kv, i: (i, col + kv))

    def ctx(col):
        return pl.BlockSpec((n_ctx, HEAD_DIM), lambda kv, i: (s_lat // n_ctx, col + kv))

    return pl.pallas_call(
        functools.partial(_attn_b_kernel, s_lat=s_lat, tq=tq),
        grid=(B_KV_HEADS, tp // tq),
        in_specs=[
            pl.BlockSpec(memory_space=pltpu.SMEM),
            pl.BlockSpec((1, gw), lambda kv, i: (0, (A_HEADS * HEAD_DIM) // gw + kv)),
            pl.BlockSpec((tq, gw), lambda kv, i: (i, (COL_BQ * LANES) // gw + kv)),
            halo_prev(COL_BK), cur(COL_BK), halo_next(COL_BK), ctx(COL_BK),
            halo_prev(COL_BV), cur(COL_BV), halo_next(COL_BV), ctx(COL_BV),
        ],
        out_specs=pl.BlockSpec((tq, gw), lambda kv, i: (i, kv)),
        out_shape=jax.ShapeDtypeStruct((tp, B_HEADS * HEAD_DIM), BF16),
        compiler_params=_cparams(("arbitrary", "arbitrary")),
        name="attn_window",
    )(sink, beta, proj, proj, proj, proj, proj, proj, proj, proj, proj)


def _na_bias_tables(rpb, s_lat):
    rows = s_lat // GRID_W
    tr = TOKEN_TILE // GRID_W
    kh = min(NA_KH, rows)

    neg = -1e30
    n_heads = rpb.shape[0]
    per_c = []
    for c in range(GRID_W):
        c0 = min(max(c - NA_KW // 2, 0), GRID_W - NA_KW)
        lo = c0 - c + NA_KW - 1
        per_c.append(jnp.pad(rpb[:, :, lo:lo + NA_KW], ((0, 0), (0, 0), (c0, GRID_W - NA_KW - c0)),
                             constant_values=neg))
    colb = jnp.stack(per_c, axis=2).astype(F32)
    off = jnp.full((n_heads, GRID_W, GRID_W), neg, F32)

    def table(q_row0, n_rows):
        per_r = []
        for r in range(q_row0, q_row0 + tr):
            r0 = min(max(r - kh // 2, 0), n_rows - kh)
            blocks = []
            for kr in range(q_row0 - tr, q_row0 + 2 * tr):
                inside = r0 <= kr < r0 + kh and 0 <= kr < n_rows
                blocks.append(colb[:, kr - r + NA_KH - 1] if inside else off)
            per_r.append(jnp.concatenate(blocks, axis=-1))
        return jnp.stack(per_r, axis=1).reshape(n_heads, TOKEN_TILE, 3 * TOKEN_TILE)

    far = 4 * tr
    out = [table(0, rows), table(far, 2 * far + tr), table(rows - tr, rows)]
    out.append(jnp.full_like(out[0], neg))
    return jnp.stack(out, axis=0)


def _attn_c_kernel(beta_ref, bias_ref, q_ref, kp_ref, kc_ref, kn_ref, kx_ref, vp_ref, vc_ref, vn_ref, vx_ref,
                   o_ref, *, tq):
    scale = HEAD_DIM ** -0.5
    q = q_ref[...]
    s_x = _dot_nt(q, kx_ref[...]) * scale
    s_p = _dot_nt(q, kp_ref[...]) * scale + bias_ref[0, 0, :, 0:tq]
    s_c = _dot_nt(q, kc_ref[...]) * scale + bias_ref[0, 0, :, tq:2 * tq]
    s_n = _dot_nt(q, kn_ref[...]) * scale + bias_ref[0, 0, :, 2 * tq:3 * tq]
    mx = jnp.maximum(jnp.maximum(jnp.max(s_x, -1, keepdims=True), jnp.max(s_p, -1, keepdims=True)),
                     jnp.maximum(jnp.max(s_c, -1, keepdims=True), jnp.max(s_n, -1, keepdims=True)))
    e_x = jnp.exp(s_x - mx)
    e_p = jnp.exp(s_p - mx)
    e_c = jnp.exp(s_c - mx)
    e_n = jnp.exp(s_n - mx)
    den = (jnp.sum(e_x, -1, keepdims=True) + jnp.sum(e_p, -1, keepdims=True)
           + jnp.sum(e_c, -1, keepdims=True) + jnp.sum(e_n, -1, keepdims=True))
    o = (jnp.dot(e_x.astype(BF16), vx_ref[...], preferred_element_type=F32)
         + jnp.dot(e_p.astype(BF16), vp_ref[...], preferred_element_type=F32)
         + jnp.dot(e_c.astype(BF16), vc_ref[...], preferred_element_type=F32)
         + jnp.dot(e_n.astype(BF16), vn_ref[...], preferred_element_type=F32))
    o_ref[...] = ((o / den) * beta_ref[...]).astype(o_ref.dtype)


def _attention_c(proj, bias_tabs, beta, s_lat, n_ctx):
    tp = proj.shape[0]
    tq = TOKEN_TILE
    n_lat_tiles = s_lat // tq
    n_tiles = tp // tq
    beta_col0 = (A_HEADS + B_HEADS)

    def kind(i):
        return jnp.where(i >= n_lat_tiles, 3, jnp.where(i == 0, 0, jnp.where(i == n_lat_tiles - 1, 2, 1)))

    def prev(col):
        return pl.BlockSpec((tq, HEAD_DIM), lambda h, i: (jnp.maximum(i - 1, 0), col + h))

    def cur(col):
        return pl.BlockSpec((tq, HEAD_DIM), lambda h, i: (i, col + h))

    def nxt(col):
        return pl.BlockSpec((tq, HEAD_DIM), lambda h, i: (jnp.minimum(i + 1, n_tiles - 1), col + h))

    def ctx(col):
        return pl.BlockSpec((n_ctx, HEAD_DIM), lambda h, i: (s_lat // n_ctx, col + h))

    return pl.pallas_call(
        functools.partial(_attn_c_kernel, tq=tq),
        grid=(C_HEADS, n_tiles),
        in_specs=[
            pl.BlockSpec((1, HEAD_DIM), lambda h, i: (0, beta_col0 + h)),
            pl.BlockSpec((1, 1, tq, 3 * tq), lambda h, i: (kind(i), h, 0, 0)),
            cur(COL_CQ),
            prev(COL_CK), cur(COL_CK), nxt(COL_CK), ctx(COL_CK),
            prev(COL_CV), cur(COL_CV), nxt(COL_CV), ctx(COL_CV),
        ],
        out_specs=pl.BlockSpec((tq, HEAD_DIM), lambda h, i: (i, h)),
        out_shape=jax.ShapeDtypeStruct((tp, C_HEADS * HEAD_DIM), BF16),
        compiler_params=_cparams(("arbitrary", "arbitrary")),
        name="attn_neighbourhood",
    )(beta, bias_tabs, proj, proj, proj, proj, proj, proj, proj, proj, proj)


def _outproj_kernel(ma_ref, mb_ref, mc_ref, w_ref, x_ref, gate_ref, g_ref, b_ref, o_ref, *, alpha):
    wa = A_HEADS * HEAD_DIM
    wb = wa + B_HEADS * HEAD_DIM
    y = (jnp.dot(ma_ref[...], w_ref[0:wa, :], preferred_element_type=F32)
         + jnp.dot(mb_ref[...], w_ref[wa:wb, :], preferred_element_type=F32)
         + jnp.dot(mc_ref[...], w_ref[wb:, :], preferred_element_type=F32))
    z = alpha * x_ref[...] + gate_ref[0] * y
    o_ref[...] = _layer_norm(z, g_ref[...], b_ref[...])


def _out_projection(ma, mb, mc, w_out, xs, modv, layer, n_lat_tiles, ln_g, ln_b, alpha):
    tp, d = xs.shape
    tm = TOKEN_TILE // 2
    n_lat = n_lat_tiles * (TOKEN_TILE // tm)
    row = lambda i: (i, 0)
    fixed = lambda i: (0, 0)
    return pl.pallas_call(
        functools.partial(_outproj_kernel, alpha=alpha),
        grid=(tp // tm,),
        in_specs=[
            pl.BlockSpec((tm, ma.shape[1]), row),
            pl.BlockSpec((tm, mb.shape[1]), row),
            pl.BlockSpec((tm, mc.shape[1]), row),
            pl.BlockSpec(w_out.shape, fixed, pipeline_mode=pl.Buffered(1)),
            pl.BlockSpec((tm, d), row),
            pl.BlockSpec((1, 1, d), _mod_index(layer, 2, n_lat)),
            pl.BlockSpec((1, d), fixed),
            pl.BlockSpec((1, d), fixed),
        ],
        out_specs=pl.BlockSpec((tm, d), row),
        out_shape=jax.ShapeDtypeStruct((tp, d), F32),
        compiler_params=_cparams(("arbitrary",)),
        name="out_projection_ln",
    )(ma, mb, mc, w_out, xs, modv, ln_g, ln_b)


def _take_top(vals, rounds, want_rank):
    tops, cnts = [], []
    n, t = vals.shape
    rank = jnp.full((n, t), float(n), F32) if want_rank else None
    seen = jnp.zeros((1, t), F32)
    for _ in range(rounds):
        m = jnp.max(vals, axis=0, keepdims=True)
        eq = vals == m
        cnt = jnp.sum(jnp.where(eq, 1.0, 0.0), axis=0, keepdims=True)
        if want_rank:
            rank = jnp.minimum(rank, jnp.where(eq, seen, float(n)))
            seen = seen + cnt
        tops.append(m)
        cnts.append(cnt)
        vals = jnp.where(eq, -jnp.inf, vals)
    return jnp.concatenate(tops, axis=0), jnp.concatenate(cnts, axis=0), rank


def _pair_table(a, b, op):
    half = PEER_TOPK // 2
    pieces = [op(a[0:1], b)] + [op(a[r:r + 1], b[0:half]) for r in range(1, half)] + [op(a[half:], b[0:1])]
    return jnp.concatenate(pieces, axis=0)


def _route_kernel(q_ref, keys_ref, n1_ref, rk2_ref, p1_ref, p2_ref, s_sc):
    k = PEER_TOPK
    tt = q_ref.shape[0]
    for h in range(PEER_HEADS):
        s_sc[0] = _dot_nt(keys_ref[h, 0], q_ref[:, (2 * h) * LANES:(2 * h + 1) * LANES])
        s_sc[1] = _dot_nt(keys_ref[h, 1], q_ref[:, (2 * h + 1) * LANES:(2 * h + 2) * LANES])
        for lc in range(tt // LANES):
            cols = slice(lc * LANES, (lc + 1) * LANES)
            s1 = s_sc[0, :, cols]
            s2 = s_sc[1, :, cols]
            v1, c1, _ = _take_top(s1, k, False)
            v2, c2, rk2 = _take_top(s2, k, True)
            cand = _pair_table(v1, v2, jnp.add)
            mult = _pair_table(c1, c2, jnp.multiply)
            thr = jnp.full((1, LANES), jnp.inf, F32)
            seen = jnp.zeros((1, LANES), F32)
            rem = cand
            for _ in range(k):
                m = jnp.max(rem, axis=0, keepdims=True)
                eq = rem == m
                thr = jnp.where(seen < k, m, thr)
                seen = seen + jnp.sum(jnp.where(eq, mult, 0.0), axis=0, keepdims=True)
                rem = jnp.where(eq, -jnp.inf, rem)
            pair = _pair_table(jnp.exp(v1 - v1[0:1]), jnp.exp(v2 - v2[0:1]), jnp.multiply)
            z = jnp.sum(jnp.where(cand >= thr, mult * pair, 0.0), axis=0, keepdims=True)
            n1 = jnp.zeros_like(s1)
            for r in range(k):
                n1 = n1 + jnp.where(s1 + v2[r:r + 1] >= thr, c2[r:r + 1], 0.0)
            n1_ref[h, :, cols] = n1
            rk2_ref[h, :, cols] = rk2.astype(BF16)
            p1_ref[h, :, cols] = jnp.exp(s1 - v1[0:1])
            p2_ref[h, :, cols] = (jnp.exp(s2 - v2[0:1]) / z).astype(BF16)


def _peer_route(qp, keys):
    tp = qp.shape[0]
    tt = TOKEN_TILE
    big = pl.BlockSpec((PEER_HEADS, PEER_KEYS, tt), lambda j: (0, 0, j))
    wide = jax.ShapeDtypeStruct((PEER_HEADS, PEER_KEYS, tp), F32)
    narrow = jax.ShapeDtypeStruct((PEER_HEADS, PEER_KEYS, tp), BF16)
    return pl.pallas_call(
        _route_kernel,
        grid=(tp // tt,),
        in_specs=[
            pl.BlockSpec((tt, qp.shape[1]), lambda j: (j, 0)),
            pl.BlockSpec(keys.shape, lambda j: (0, 0, 0, 0)),
        ],
        out_specs=[big, big, big, big],
        out_shape=[wide, narrow, wide, narrow],
        scratch_shapes=[pltpu.VMEM((2, PEER_KEYS, tt), F32)],
        compiler_params=_cparams(("arbitrary",)),
        name="peer_route",
    )(qp, keys)


def _gelu(a):
    return 0.5 * a * (1.0 + lax.erf(a * (2.0 ** -0.5)))


def _peer_expert_kernel(x_ref, sc_ref, sh_ref, gate_ref, g_ref, b_ref, u_ref, v_ref,
                        n1_ref, rk2_ref, p1_ref, p2_ref, o_ref, ux_sc, w0_sc, w1_sc, at0_sc, at1_sc, acc_sc,
                        *, alpha, te, tt, n_e):
    i = pl.program_id(1)
    pack = BF16_SUBLANES
    d = acc_sc.shape[1]
    n_blk = te // PEER_KEYS

    @pl.when(i == 0)
    def _():
        ux_sc[...] = (x_ref[...] * (1.0 + sc_ref[0]) + sh_ref[0]).astype(BF16)
        acc_sc[...] = jnp.zeros_like(acc_sc)
        w1_sc[...] = jnp.zeros_like(w1_sc)

    zero = jnp.zeros((pack, LANES), BF16)

    at_bufs = (at0_sc, at1_sc)
    eg = at0_sc.shape[0] // PEER_KEYS
    tg = at0_sc.shape[1] // LANES
    n_eg, n_tg = n_blk // eg, tt // LANES // tg
    n_pieces = n_eg * n_tg
    dn = d // n_pieces

    def hidden(r):
        ge, gt = divmod(r, n_tg)
        at_bufs[r % 2][...] = _dot_nt(u_ref[ge * eg * PEER_KEYS:(ge + 1) * eg * PEER_KEYS, :],
                                      ux_sc[gt * tg * LANES:(gt + 1) * tg * LANES, :])

    def cell(cur_sc, r, ca, cl):
        ge, gt = divmod(r, n_tg)
        a, lc = ge * eg + ca, gt * tg + cl
        cols = slice(lc * LANES, (lc + 1) * LANES)
        a_t = at_bufs[r % 2].at[ca * PEER_KEYS:(ca + 1) * PEER_KEYS, cl * LANES:(cl + 1) * LANES]
        n1 = [jnp.broadcast_to(n1_ref[h, a:a + 1, cols], (pack, LANES)).astype(BF16) for h in range(PEER_HEADS)]
        p1 = [jnp.broadcast_to(p1_ref[h, a:a + 1, cols], (pack, LANES)).astype(BF16) for h in range(PEER_HEADS)]
        pieces = []
        for rg in range(PEER_KEYS // pack):
            rows = slice(rg * pack, (rg + 1) * pack)
            g = zero
            for h in range(PEER_HEADS):
                sel_p1 = jnp.minimum(jnp.maximum(n1[h] - rk2_ref[h, rows, cols], zero), p1[h])
                g = g + sel_p1 * p2_ref[h, rows, cols]
            pieces.append(g * _gelu(a_t[rows, :]).astype(BF16))
        blk = jnp.concatenate(pieces, axis=0)
        cur_sc[cols, a * PEER_KEYS:(a + 1) * PEER_KEYS] = blk.T

    def step(cur_sc, prev_sc):
        hidden(0)
        for r in range(n_pieces):
            cells = [(ca, cl) for ca in range(eg) for cl in range(tg)]
            half = len(cells) // 2
            if r + 1 < n_pieces:
                hidden(r + 1)
            for ca, cl in cells[:half]:
                cell(cur_sc, r, ca, cl)
            acc_sc[:, r * dn:(r + 1) * dn] += jnp.dot(prev_sc[...], v_ref[:, r * dn:(r + 1) * dn],
                                                      preferred_element_type=F32)
            for ca, cl in cells[half:]:
                cell(cur_sc, r, ca, cl)

    @pl.when((i < n_e) & (i % 2 == 0))
    def _():
        step(w0_sc, w1_sc)

    @pl.when((i < n_e) & (i % 2 == 1))
    def _():
        step(w1_sc, w0_sc)

    @pl.when(i == n_e)
    def _():
        last_sc = w1_sc if n_e % 2 == 0 else w0_sc
        y = acc_sc[...] + jnp.dot(last_sc[...], v_ref[...], preferred_element_type=F32)
        z = alpha * x_ref[...] + gate_ref[0] * y
        o_ref[...] = _layer_norm(z, g_ref[...], b_ref[...])


def _peer_experts(xs, modv, layer, n_lat_tiles, n_out_tiles, ln_g, ln_b, u, v, route, alpha):
    tp, d = xs.shape
    tt = TOKEN_TILE
    te = 8 * PEER_KEYS
    n_exp = u.shape[0]
    n1, rk2, p1, p2 = route
    once = pl.Buffered(1)
    tok = lambda j, i: (j, 0)
    fixed = lambda j, i: (0, 0)
    mod = lambda chunk: (lambda j, i: _mod_index(layer, chunk, n_lat_tiles)(j))
    rt = lambda: pl.BlockSpec((PEER_HEADS, PEER_KEYS, tt), lambda j, i: (0, 0, j), pipeline_mode=once)
    n_e = n_exp // te
    build = lambda i: jnp.minimum(i, n_e - 1)
    drain = lambda i: jnp.maximum(i - 1, 0)
    rt1 = lambda: pl.BlockSpec((PEER_HEADS, te // PEER_KEYS, tt), lambda j, i: (0, build(i), j))
    return pl.pallas_call(
        functools.partial(_peer_expert_kernel, alpha=alpha, te=te, tt=tt, n_e=n_e),
        grid=(n_out_tiles, n_e + 1),
        in_specs=[
            pl.BlockSpec((tt, d), tok, pipeline_mode=once),
            pl.BlockSpec((1, 1, d), mod(4)),
            pl.BlockSpec((1, 1, d), mod(3)),
            pl.BlockSpec((1, 1, d), mod(5)),
            pl.BlockSpec((1, d), fixed),
            pl.BlockSpec((1, d), fixed),
            pl.BlockSpec((te, d), lambda j, i: (build(i), 0)),
            pl.BlockSpec((te, d), lambda j, i: (drain(i), 0)),
            rt1(), rt(), rt1(), rt(),
        ],
        out_specs=pl.BlockSpec((tt, d), tok),
        out_shape=jax.ShapeDtypeStruct((n_out_tiles * tt, d), F32),
        scratch_shapes=[
            pltpu.VMEM((tt, d), BF16),
            pltpu.VMEM((tt, te), BF16),
            pltpu.VMEM((tt, te), BF16),
            pltpu.VMEM((2 * PEER_KEYS, 2 * LANES), F32),
            pltpu.VMEM((2 * PEER_KEYS, 2 * LANES), F32),
            pltpu.VMEM((tt, d), F32),
        ],
        compiler_params=_cparams(("arbitrary", "arbitrary")),
        name="peer_experts_ln",
    )(xs, modv, modv, modv, ln_g, ln_b, u, v, n1, rk2, p1, p2)


def _permute_qk_columns(w):
    d = w.shape[0]
    a_end, b_start, b_end = COL_AV * LANES, COL_BQ * LANES, COL_BV * LANES
    wa = w[:, :a_end].reshape(d, COL_AV, 2, A_QK_DIM // 2, 2)
    wa = wa.transpose(0, 1, 4, 2, 3).reshape(d, a_end)
    wb = w[:, b_start:b_end].reshape(d, COL_BV - COL_BQ, HEAD_DIM // 2, 2)
    wb = wb.transpose(0, 1, 3, 2).reshape(d, b_end - b_start)
    return jnp.concatenate([wa, w[:, a_end:b_start], wb, w[:, b_end:]], axis=1)


def _rope_tables(s_lat, tp):
    t = jnp.arange(s_lat)
    row = (t // GRID_W).astype(F32)
    col = (t % GRID_W).astype(F32)
    lane = np.arange(LANES)
    sign = jnp.asarray(np.where(lane < LANES // 2, -1.0, 1.0), F32)

    def tabs(dim, lane_to_pair):
        d_axis = dim // 2
        inv = ROPE_BASE ** (-jnp.arange(0, d_axis, 2, dtype=F32) / d_axis)
        ang = jnp.concatenate([row[:, None] * inv, col[:, None] * inv], axis=-1)
        ang = ang[:, lane_to_pair]
        pad = ((0, tp - s_lat), (0, 0))
        return (jnp.pad(jnp.cos(ang), pad, constant_values=1.0),
                jnp.pad(jnp.sin(ang) * sign, pad, constant_values=0.0))

    ca, sa = tabs(A_QK_DIM, lane % 32)
    cb, sb = tabs(HEAD_DIM, lane % 64)
    return jnp.stack([ca, cb]), jnp.stack([sa, sb])


def kernel(x, c, ctx, c_ctx, w_ada, b_ada, w_in, w_out, beta_out, ln1_g, ln1_b, ln2_g, ln2_b, diff_lq1, diff_lk1, diff_lq2, diff_lk2, diff_subln, sink, na_rpb, peer_wq, peer_keys, peer_u, peer_v):
    depth = w_ada.shape[0]
    _, s_lat, d = x.shape
    n_ctx = ctx.shape[1]
    assert x.shape[0] == 1 and s_lat % TOKEN_TILE == 0 and n_ctx == 256 and s_lat % n_ctx == 0
    n_lat_tiles = s_lat // TOKEN_TILE
    tp = (n_lat_tiles + 1) * TOKEN_TILE
    alpha = (2 * depth) ** 0.25

    xs = jnp.concatenate([x[0], ctx[0], jnp.zeros((tp - s_lat - n_ctx, d), x.dtype)], axis=0)
    cc = jnp.concatenate([c[0:1], c_ctx[None, :], jnp.zeros((6, d), c.dtype)], axis=0)
    mod = _ada_modulation(cc, w_ada, b_ada)
    modv = mod[:, :2, :].reshape(depth * 2 * 6, 1, d)
    rope_tabs = _rope_tables(s_lat, tp)

    for l in range(depth):
        lam_init = 0.8 - 0.6 * math.exp(-0.3 * l)
        last = l == depth - 1
        w_in_l = _permute_qk_columns(w_in[l].astype(BF16))
        proj, vt = _projection(xs, modv, l, 1, 0, w_in_l, n_lat_tiles, rope_tabs)
        beta = beta_out[l][None, :]
        ma = _attention_a(proj, vt, diff_lq1[l][None], diff_lk1[l][None], diff_lq2[l][None], diff_lk2[l][None],
                          diff_subln[l][None], beta, s_lat, n_ctx, lam_init)
        mb = _attention_b(proj, sink[l], beta, s_lat, n_ctx)
        mc = _attention_c(proj, _na_bias_tables(na_rpb[l], s_lat), beta, s_lat, n_ctx)
        xs = _out_projection(ma, mb, mc, w_out[l].astype(BF16), xs, modv, l, n_lat_tiles,
                             ln1_g[l][None], ln1_b[l][None], alpha)
        qp = _projection(xs, modv, l, 4, 3, peer_wq[l].astype(BF16), n_lat_tiles)
        route = _peer_route(qp, peer_keys[l].astype(BF16))
        xs = _peer_experts(xs, modv, l, n_lat_tiles, n_lat_tiles if last else n_lat_tiles + 1,
                           ln2_g[l][None], ln2_b[l][None], peer_u[l].astype(BF16), peer_v[l].astype(BF16),
                           route, alpha)
    return xs[None]
```

```python
import functools
import math

import numpy as np
import jax
import jax.numpy as jnp
from jax import lax
from jax.experimental import pallas as pl
from jax.experimental.pallas import tpu as pltpu

F32 = jnp.float32
BF16 = jnp.bfloat16

GRID_W = 64
HEAD_DIM = 128
A_HEADS = 4
B_HEADS = 8
B_KV_HEADS = 2
B_GROUP = 4
C_HEADS = 4
A_QK_DIM = 64
WINDOW = 128
NA_KH = 8
NA_KW = 16
NA_HALO_ROWS = NA_KH // 2
PEER_HEADS = 8
PEER_KEYS = 128
PEER_TOPK = 16
ROPE_BASE = 10000.0
LN_EPS = 1e-5

LANES = 128
BF16_SUBLANES = 16
TOKEN_TILE = 512
KEY_CHUNK = 256
A_ONES_ROWS = 16
A_Q_SCALE = A_QK_DIM ** -0.5 * math.log2(math.e)
VMEM_LIMIT = 56 * 1024 * 1024

COL_AQ, COL_AK, COL_AV = 0, 4, 8
COL_BQ, COL_BK, COL_BV = 12, 20, 22
COL_CQ, COL_CK, COL_CV = 24, 28, 32
N_COLBLOCKS = 36


def _cparams(sem):
    return pltpu.CompilerParams(dimension_semantics=sem, vmem_limit_bytes=VMEM_LIMIT)


def _dot_nt(a, b):
    return lax.dot_general(a, b, (((1,), (1,)), ((), ())), preferred_element_type=F32)


def _dot_tn(a, b):
    return lax.dot_general(a, b, (((0,), (0,)), ((), ())), preferred_element_type=F32)


def _layer_norm(y, g, b):
    mu = jnp.mean(y, axis=-1, keepdims=True)
    d = y - mu
    var = jnp.mean(d * d, axis=-1, keepdims=True)
    return d * lax.rsqrt(var + LN_EPS) * g + b


def _ada_kernel(c_ref, w_ref, b_ref, o_ref):
    c = c_ref[...]
    a = c * jax.nn.sigmoid(c)
    o_ref[0] = jnp.dot(a.astype(BF16), w_ref[0].astype(BF16), preferred_element_type=F32) + b_ref[0]


def _ada_modulation(cc, w_ada, b_ada):
    depth, d, n = w_ada.shape
    tn = 1024
    return pl.pallas_call(
        _ada_kernel,
        grid=(depth, n // tn),
        in_specs=[
            pl.BlockSpec((8, d), lambda l, j: (0, 0)),
            pl.BlockSpec((1, d, tn), lambda l, j: (l, 0, j)),
            pl.BlockSpec((1, 1, tn), lambda l, j: (l, 0, j)),
        ],
        out_specs=pl.BlockSpec((1, 8, tn), lambda l, j: (l, 0, j)),
        out_shape=jax.ShapeDtypeStruct((depth, 8, n), F32),
        compiler_params=_cparams(("arbitrary", "arbitrary")),
        name="ada_modulation",
    )(cc, w_ada, b_ada.reshape(depth, 1, n))


def _rope_kind(colblock):
    if colblock < COL_AV:
        return 0
    if COL_BQ <= colblock < COL_BV:
        return 1
    return None


def _proj_kernel(x_ref, sc_ref, sh_ref, w_ref, *rest, tn, rope):
    if rope:
        cos_ref, sin_ref, o_ref, vt_ref = rest
    else:
        (o_ref,) = rest
    xm = (x_ref[...] * (1.0 + sc_ref[0]) + sh_ref[0]).astype(BF16)
    n = w_ref.shape[1]
    for jn in range(n // tn):
        y = jnp.dot(xm, w_ref[:, jn * tn:(jn + 1) * tn], preferred_element_type=F32)
        for jb in range(tn // LANES):
            colblock = jn * (tn // LANES) + jb
            yb = y[:, jb * LANES:(jb + 1) * LANES]
            kind = _rope_kind(colblock) if rope else None
            if kind is not None:
                yb = yb * cos_ref[kind] + pltpu.roll(yb, LANES // 2, 1) * sin_ref[kind]
            if rope and colblock < COL_AK:
                yb = yb * A_Q_SCALE
            o_ref[:, colblock * LANES:(colblock + 1) * LANES] = yb.astype(o_ref.dtype)
            if rope and COL_AV <= colblock < COL_BQ:
                hd = colblock - COL_AV
                for ck in range(yb.shape[0] // KEY_CHUNK):
                    vt_ref[ck, hd * HEAD_DIM:(hd + 1) * HEAD_DIM, :] = (
                        yb[ck * KEY_CHUNK:(ck + 1) * KEY_CHUNK, :].T.astype(vt_ref.dtype))


def _mod_index(layer_slot, chunk, n_lat_tiles):
    def index(i):
        who = jnp.where(i >= n_lat_tiles, 1, 0)
        return ((layer_slot * 2 + who) * 6 + chunk, 0, 0)
    return index


def _projection(xs, modv, layer, chunk_scale, chunk_shift, w, n_lat_tiles, rope_tabs=None):
    tp, d = xs.shape
    n = w.shape[1]
    tm = TOKEN_TILE
    rope = rope_tabs is not None
    in_specs = [
        pl.BlockSpec((tm, d), lambda i: (i, 0)),
        pl.BlockSpec((1, 1, d), _mod_index(layer, chunk_scale, n_lat_tiles)),
        pl.BlockSpec((1, 1, d), _mod_index(layer, chunk_shift, n_lat_tiles)),
        pl.BlockSpec((d, n), lambda i: (0, 0), pipeline_mode=pl.Buffered(1)),
    ]
    args = [xs, modv, modv, w]
    if rope:
        in_specs += [pl.BlockSpec((2, tm, LANES), lambda i: (0, i, 0))] * 2
        args += list(rope_tabs)
    out_specs = pl.BlockSpec((tm, n), lambda i: (i, 0))
    out_shape = jax.ShapeDtypeStruct((tp, n), BF16)
    if rope:
        per = tm // KEY_CHUNK
        out_specs = [out_specs, pl.BlockSpec((per, A_HEADS * HEAD_DIM, KEY_CHUNK), lambda i: (i, 0, 0))]
        out_shape = [out_shape, jax.ShapeDtypeStruct((tp // KEY_CHUNK, A_HEADS * HEAD_DIM, KEY_CHUNK), BF16)]
    return pl.pallas_call(
        functools.partial(_proj_kernel, tn=512, rope=rope),
        grid=(tp // tm,),
        in_specs=in_specs,
        out_specs=out_specs,
        out_shape=out_shape,
        compiler_params=_cparams(("arbitrary",)),
        name="mod_projection_rope" if rope else "mod_projection",
    )(*args)


def _attn_a_kernel(lq1_ref, lk1_ref, lq2_ref, lk2_ref, subln_ref, beta_ref, q_ref, k_ref, vt_ref, o_ref,
                   qs_sc, s0_sc, s1_sc, p0_sc, p1_sc, m_sc, acc_sc, *, s_lat, tq, n_sub, n_lat_tiles, lam_init):
    i = pl.program_id(1)
    lane = lax.broadcasted_iota(jnp.int32, (1, LANES), 1)
    map1 = ((lane // 32) % 2) == 1
    q = q_ref[...]
    zero = jnp.zeros_like(q)
    qs_sc[0:tq, :] = jnp.where(map1, zero, q)
    qs_sc[tq:2 * tq, :] = jnp.where(map1, q, zero)
    m_sc[...] = jnp.full_like(m_sc, -jnp.inf)
    acc_sc[...] = jnp.zeros_like(acc_sc)
    ones_rows = jnp.where(lax.broadcasted_iota(jnp.int32, (A_ONES_ROWS, KEY_CHUNK), 0) == 0, 1.0, 0.0).astype(BF16)

    def keys(c):
        start = c * KEY_CHUNK
        if not isinstance(c, int):
            start = pl.multiple_of(start, KEY_CHUNK)
        return k_ref[pl.ds(start, KEY_CHUNK), :]

    def scores(group, s_sc):
        for j in range(n_sub):
            s_sc[j] = _dot_nt(keys(group * n_sub + j), qs_sc[...])

    def update(chunks, s):
        m_prev = m_sc[...]
        m_new = m_prev
        for sj in s:
            m_new = jnp.maximum(m_new, jnp.max(sj, axis=0, keepdims=True))
        acc = jnp.exp2(m_prev - m_new) * acc_sc[...]
        for c, sj in zip(chunks, s):
            p = jnp.exp2(sj - m_new).astype(BF16)
            v1 = jnp.concatenate([vt_ref[c], ones_rows], axis=0)
            acc = acc + jnp.dot(v1, p, preferred_element_type=F32)
        acc_sc[...] = acc
        m_sc[...] = m_new

    ctx_chunk = s_lat // KEY_CHUNK
    update([ctx_chunk], [_dot_nt(keys(ctx_chunk), qs_sc[...])])

    def values(group, p_sc):
        out = None
        for j in range(n_sub):
            v1 = jnp.concatenate([vt_ref[group * n_sub + j], ones_rows], axis=0)
            part = jnp.dot(v1, p_sc[j], preferred_element_type=F32)
            out = part if out is None else out + part
        return out

    def stage(g_scores, s_next_sc, g_values, p_prev_sc, s_cur_sc, p_cur_sc):
        acc_sc[...] += values(g_values, p_prev_sc)
        scores(g_scores, s_next_sc)
        m_prev = m_sc[...]
        m_new = m_prev
        for j in range(n_sub):
            m_new = jnp.maximum(m_new, jnp.max(s_cur_sc[j], axis=0, keepdims=True))
        for j in range(n_sub):
            p_cur_sc[j] = jnp.exp2(s_cur_sc[j] - m_new).astype(BF16)
        acc_sc[...] *= jnp.exp2(m_prev - m_new)
        m_sc[...] = m_new

    @pl.when(i < n_lat_tiles)
    def _():
        n_groups = s_lat // (KEY_CHUNK * n_sub)
        scores(0, s0_sc)
        p1_sc[...] = jnp.zeros_like(p1_sc)

        unroll = math.gcd(n_groups, 8)
        assert unroll % 2 == 0

        def body(gu, carry):
            for u in range(unroll):
                g = unroll * gu + u
                bufs = (s1_sc, p1_sc, s0_sc, p0_sc) if u % 2 == 0 else (s0_sc, p0_sc, s1_sc, p1_sc)
                s_next, p_prev, s_cur, p_cur = bufs
                stage(jnp.minimum(g + 1, n_groups - 1), s_next, jnp.maximum(g - 1, 0), p_prev, s_cur, p_cur)
            return carry
        lax.fori_loop(0, n_groups // unroll, body, 0)
        acc_sc[...] += values(n_groups - 1, p1_sc)

    lam = (jnp.exp(jnp.sum(lq1_ref[...] * lk1_ref[...], axis=-1, keepdims=True))
           - jnp.exp(jnp.sum(lq2_ref[...] * lk2_ref[...], axis=-1, keepdims=True)) + lam_init)
    l = acc_sc[HEAD_DIM:HEAD_DIM + 1, :]
    o_t = (acc_sc[0:HEAD_DIM, 0:tq] / l[:, 0:tq] - lam * (acc_sc[0:HEAD_DIM, tq:2 * tq] / l[:, tq:2 * tq]))
    o = o_t.T
    y = o * lax.rsqrt(jnp.mean(o * o, axis=-1, keepdims=True) + LN_EPS) * subln_ref[...]
    o_ref[...] = (y * (1.0 - lam_init) * beta_ref[...]).astype(o_ref.dtype)


def _attention_a(proj, vt, lq1, lk1, lq2, lk2, subln, beta, s_lat, n_ctx, lam_init):
    tp = proj.shape[0]
    assert n_ctx == KEY_CHUNK
    tq, n_sub = 256, 2
    n_lat_tiles = s_lat // tq
    vec = lambda: pl.BlockSpec((1, A_QK_DIM), lambda h, i: (0, 0))
    return pl.pallas_call(
        functools.partial(_attn_a_kernel, s_lat=s_lat, tq=tq, n_sub=n_sub, n_lat_tiles=n_lat_tiles,
                          lam_init=lam_init),
        grid=(A_HEADS, tp // tq),
        in_specs=[
            vec(), vec(), vec(), vec(),
            pl.BlockSpec((1, HEAD_DIM), lambda h, i: (0, 0)),
            pl.BlockSpec((1, HEAD_DIM), lambda h, i: (0, h)),
            pl.BlockSpec((tq, HEAD_DIM), lambda h, i: (i, COL_AQ + h)),
            pl.BlockSpec((tp, HEAD_DIM), lambda h, i: (0, COL_AK + h)),
            pl.BlockSpec((tp // KEY_CHUNK, HEAD_DIM, KEY_CHUNK), lambda h, i: (0, h, 0)),
        ],
        out_specs=pl.BlockSpec((tq, HEAD_DIM), lambda h, i: (i, h)),
        out_shape=jax.ShapeDtypeStruct((tp, A_HEADS * HEAD_DIM), BF16),
        scratch_shapes=[
            pltpu.VMEM((2 * tq, HEAD_DIM), BF16),
            pltpu.VMEM((n_sub, KEY_CHUNK, 2 * tq), F32),
            pltpu.VMEM((n_sub, KEY_CHUNK, 2 * tq), F32),
            pltpu.VMEM((n_sub, KEY_CHUNK, 2 * tq), BF16),
            pltpu.VMEM((n_sub, KEY_CHUNK, 2 * tq), BF16),
            pltpu.VMEM((1, 2 * tq), F32),
            pltpu.VMEM((HEAD_DIM + A_ONES_ROWS, 2 * tq), F32),
        ],
        compiler_params=_cparams(("arbitrary", "arbitrary")),
        name="attn_diff",
    )(lq1, lk1, lq2, lk2, subln, beta, proj, proj, vt)


def _attn_b_kernel(sink_ref, beta_ref, q_ref, kp_ref, kc_ref, kn_ref, kx_ref, vp_ref, vc_ref, vn_ref, vx_ref,
                   o_ref, *, s_lat, tq):
    kv = pl.program_id(0)
    i = pl.program_id(1)
    scale = HEAD_DIM ** -0.5
    q0 = i * tq
    qpos = q0 + lax.broadcasted_iota(jnp.int32, (tq, 1), 0)

    def valid(kstart, n):
        kpos = kstart + lax.broadcasted_iota(jnp.int32, (1, n), 1)
        return (jnp.abs(kpos - qpos) <= WINDOW) & (kpos >= 0) & (kpos < s_lat) & (qpos < s_lat)

    ok_p = valid(q0 - WINDOW, WINDOW)
    ok_c = valid(q0, tq)
    ok_n = valid(q0 + tq, WINDOW)
    neg = -jnp.inf
    for g in range(B_GROUP):
        qg = q_ref[:, g * HEAD_DIM:(g + 1) * HEAD_DIM]
        s_x = _dot_nt(qg, kx_ref[...]) * scale
        s_p = jnp.where(ok_p, _dot_nt(qg, kp_ref[...]) * scale, neg)
        s_c = jnp.where(ok_c, _dot_nt(qg, kc_ref[...]) * scale, neg)
        s_n = jnp.where(ok_n, _dot_nt(qg, kn_ref[...]) * scale, neg)
        snk = sink_ref[kv * B_GROUP + g]
        mx = jnp.maximum(jnp.maximum(jnp.max(s_x, -1, keepdims=True), jnp.max(s_p, -1, keepdims=True)),
                         jnp.maximum(jnp.max(s_c, -1, keepdims=True), jnp.max(s_n, -1, keepdims=True)))
        mx = jnp.maximum(mx, snk)
        e_x = jnp.exp(s_x - mx)
        e_p = jnp.exp(s_p - mx)
        e_c = jnp.exp(s_c - mx)
        e_n = jnp.exp(s_n - mx)
        den = (jnp.exp(snk - mx) + jnp.sum(e_x, -1, keepdims=True) + jnp.sum(e_p, -1, keepdims=True)
               + jnp.sum(e_c, -1, keepdims=True) + jnp.sum(e_n, -1, keepdims=True))
        o = (jnp.dot(e_x.astype(BF16), vx_ref[...], preferred_element_type=F32)
             + jnp.dot(e_p.astype(BF16), vp_ref[...], preferred_element_type=F32)
             + jnp.dot(e_c.astype(BF16), vc_ref[...], preferred_element_type=F32)
             + jnp.dot(e_n.astype(BF16), vn_ref[...], preferred_element_type=F32))
        o_ref[:, g * HEAD_DIM:(g + 1) * HEAD_DIM] = (
            (o / den) * beta_ref[:, g * HEAD_DIM:(g + 1) * HEAD_DIM]).astype(o_ref.dtype)


def _attention_b(proj, sink, beta, s_lat, n_ctx):
    tp = proj.shape[0]
    tq = 256
    per = tq // WINDOW
    last_halo = tp // WINDOW - 1
    gw = B_GROUP * HEAD_DIM

    def halo_prev(col):
        return pl.BlockSpec((WINDOW, HEAD_DIM), lambda kv, i: (jnp.maximum(i * per - 1, 0), col + kv))

    def halo_next(col):
        return pl.BlockSpec((WINDOW, HEAD_DIM), lambda kv, i: (jnp.minimum((i + 1) * per, last_halo), col + kv))

    def cur(col):
        return pl.BlockSpec((tq, HEAD_DIM), lambda kv, i: (i, col + kv))

    def ctx(col):
        return pl.BlockSpec((n_ctx, HEAD_DIM), lambda kv, i: (s_lat // n_ctx, col + kv))

    return pl.pallas_call(
        functools.partial(_attn_b_kernel, s_lat=s_lat, tq=tq),
        grid=(B_KV_HEADS, tp // tq),
        in_specs=[
            pl.BlockSpec(memory_space=pltpu.SMEM),
            pl.BlockSpec((1, gw), lambda kv, i: (0, (A_HEADS * HEAD_DIM) // gw + kv)),
            pl.BlockSpec((tq, gw), lambda kv, i: (i, (COL_BQ * LANES) // gw + kv)),
            halo_prev(COL_BK), cur(COL_BK), halo_next(COL_BK), ctx(COL_BK),
            halo_prev(COL_BV), cur(COL_BV), halo_next(COL_BV), ctx(COL_BV),
        ],
        out_specs=pl.BlockSpec((tq, gw), lambda kv, i: (i, kv)),
        out_shape=jax.ShapeDtypeStruct((tp, B_HEADS * HEAD_DIM), BF16),
        compiler_params=_cparams(("arbitrary", "arbitrary")),
        name="attn_window",
    )(sink, beta, proj, proj, proj, proj, proj, proj, proj, proj, proj)


def _na_bias_tables(rpb, s_lat):
    rows = s_lat // GRID_W
    tr = TOKEN_TILE // GRID_W
    kh = min(NA_KH, rows)

    neg = -1e30
    n_heads = rpb.shape[0]
    per_c = []
    for c in range(GRID_W):
        c0 = min(max(c - NA_KW // 2, 0), GRID_W - NA_KW)
        lo = c0 - c + NA_KW - 1
        per_c.append(jnp.pad(rpb[:, :, lo:lo + NA_KW], ((0, 0), (0, 0), (c0, GRID_W - NA_KW - c0)),
                             constant_values=neg))
    colb = jnp.stack(per_c, axis=2).astype(F32)
    off = jnp.full((n_heads, GRID_W, GRID_W), neg, F32)

    hr = NA_HALO_ROWS

    def table(q_row0, n_rows):
        per_r = []
        for r in range(q_row0, q_row0 + tr):
            r0 = min(max(r - kh // 2, 0), n_rows - kh)
            assert q_row0 - hr <= r0 and r0 + kh <= q_row0 + tr + hr
            blocks = []
            for kr in range(q_row0 - hr, q_row0 + tr + hr):
                inside = r0 <= kr < r0 + kh and 0 <= kr < n_rows
                blocks.append(colb[:, kr - r + NA_KH - 1] if inside else off)
            per_r.append(jnp.concatenate(blocks, axis=-1))
        return jnp.stack(per_r, axis=1).reshape(n_heads, TOKEN_TILE, TOKEN_TILE + 2 * hr * GRID_W)

    far = 4 * tr
    out = [table(0, rows), table(far, 2 * far + tr), table(rows - tr, rows)]
    out.append(jnp.full_like(out[0], neg))
    return jnp.stack(out, axis=0)


def _attn_c_kernel(beta_ref, bias_ref, q_ref, kp_ref, kc_ref, kn_ref, kx_ref, vp_ref, vc_ref, vn_ref, vx_ref,
                   o_ref, *, tq):
    scale = HEAD_DIM ** -0.5
    q = q_ref[...]
    hq = kp_ref.shape[0]
    s_x = _dot_nt(q, kx_ref[...]) * scale
    s_p = _dot_nt(q, kp_ref[...]) * scale + bias_ref[0, 0, :, 0:hq]
    s_c = _dot_nt(q, kc_ref[...]) * scale + bias_ref[0, 0, :, hq:hq + tq]
    s_n = _dot_nt(q, kn_ref[...]) * scale + bias_ref[0, 0, :, hq + tq:2 * hq + tq]
    mx = jnp.maximum(jnp.maximum(jnp.max(s_x, -1, keepdims=True), jnp.max(s_p, -1, keepdims=True)),
                     jnp.maximum(jnp.max(s_c, -1, keepdims=True), jnp.max(s_n, -1, keepdims=True)))
    e_x = jnp.exp(s_x - mx)
    e_p = jnp.exp(s_p - mx)
    e_c = jnp.exp(s_c - mx)
    e_n = jnp.exp(s_n - mx)
    den = (jnp.sum(e_x, -1, keepdims=True) + jnp.sum(e_p, -1, keepdims=True)
           + jnp.sum(e_c, -1, keepdims=True) + jnp.sum(e_n, -1, keepdims=True))
    o = (jnp.dot(e_x.astype(BF16), vx_ref[...], preferred_element_type=F32)
         + jnp.dot(e_p.astype(BF16), vp_ref[...], preferred_element_type=F32)
         + jnp.dot(e_c.astype(BF16), vc_ref[...], preferred_element_type=F32)
         + jnp.dot(e_n.astype(BF16), vn_ref[...], preferred_element_type=F32))
    o_ref[...] = ((o / den) * beta_ref[...]).astype(o_ref.dtype)


def _attention_c(proj, bias_tabs, bias_head0, beta, s_lat, n_ctx):
    tp = proj.shape[0]
    tq = TOKEN_TILE
    hq = NA_HALO_ROWS * GRID_W
    per = tq // hq
    n_lat_tiles = s_lat // tq
    n_tiles = tp // tq
    beta_col0 = (A_HEADS + B_HEADS)

    def kind(i):
        return jnp.where(i >= n_lat_tiles, 3, jnp.where(i == 0, 0, jnp.where(i == n_lat_tiles - 1, 2, 1)))

    def prev(col):
        return pl.BlockSpec((hq, HEAD_DIM), lambda h, i: (jnp.maximum(i * per - 1, 0), col + h))

    def cur(col):
        return pl.BlockSpec((tq, HEAD_DIM), lambda h, i: (i, col + h))

    def nxt(col):
        return pl.BlockSpec((hq, HEAD_DIM), lambda h, i: (jnp.minimum((i + 1) * per, tp // hq - 1), col + h))

    def ctx(col):
        return pl.BlockSpec((n_ctx, HEAD_DIM), lambda h, i: (s_lat // n_ctx, col + h))

    return pl.pallas_call(
        functools.partial(_attn_c_kernel, tq=tq),
        grid=(C_HEADS, n_tiles),
        in_specs=[
            pl.BlockSpec((1, HEAD_DIM), lambda h, i: (0, beta_col0 + h)),
            pl.BlockSpec((1, 1, tq, tq + 2 * hq), lambda h, i: (kind(i), bias_head0 + h, 0, 0)),
            cur(COL_CQ),
            prev(COL_CK), cur(COL_CK), nxt(COL_CK), ctx(COL_CK),
            prev(COL_CV), cur(COL_CV), nxt(COL_CV), ctx(COL_CV),
        ],
        out_specs=pl.BlockSpec((tq, HEAD_DIM), lambda h, i: (i, h)),
        out_shape=jax.ShapeDtypeStruct((tp, C_HEADS * HEAD_DIM), BF16),
        compiler_params=_cparams(("arbitrary", "arbitrary")),
        name="attn_neighbourhood",
    )(beta, bias_tabs, proj, proj, proj, proj, proj, proj, proj, proj, proj)


def _outproj_kernel(ma_ref, mb_ref, mc_ref, w_ref, x_ref, gate_ref, g_ref, b_ref, o_ref, *, alpha):
    wa = A_HEADS * HEAD_DIM
    wb = wa + B_HEADS * HEAD_DIM
    y = (jnp.dot(ma_ref[...], w_ref[0:wa, :], preferred_element_type=F32)
         + jnp.dot(mb_ref[...], w_ref[wa:wb, :], preferred_element_type=F32)
         + jnp.dot(mc_ref[...], w_ref[wb:, :], preferred_element_type=F32))
    z = alpha * x_ref[...] + gate_ref[0] * y
    o_ref[...] = _layer_norm(z, g_ref[...], b_ref[...])


def _out_projection(ma, mb, mc, w_out, xs, modv, layer, n_lat_tiles, ln_g, ln_b, alpha):
    tp, d = xs.shape
    tm = TOKEN_TILE // 2
    n_lat = n_lat_tiles * (TOKEN_TILE // tm)
    row = lambda i: (i, 0)
    fixed = lambda i: (0, 0)
    return pl.pallas_call(
        functools.partial(_outproj_kernel, alpha=alpha),
        grid=(tp // tm,),
        in_specs=[
            pl.BlockSpec((tm, ma.shape[1]), row),
            pl.BlockSpec((tm, mb.shape[1]), row),
            pl.BlockSpec((tm, mc.shape[1]), row),
            pl.BlockSpec(w_out.shape, fixed, pipeline_mode=pl.Buffered(1)),
            pl.BlockSpec((tm, d), row),
            pl.BlockSpec((1, 1, d), _mod_index(layer, 2, n_lat)),
            pl.BlockSpec((1, d), fixed),
            pl.BlockSpec((1, d), fixed),
        ],
        out_specs=pl.BlockSpec((tm, d), row),
        out_shape=jax.ShapeDtypeStruct((tp, d), F32),
        compiler_params=_cparams(("arbitrary",)),
        name="out_projection_ln",
    )(ma, mb, mc, w_out, xs, modv, ln_g, ln_b)


def _take_top(vals, rounds, want_rank):
    tops, cnts = [], []
    n, t = vals.shape
    rank = jnp.full((n, t), float(n), F32) if want_rank else None
    seen = jnp.zeros((1, t), F32)
    for _ in range(rounds):
        m = jnp.max(vals, axis=0, keepdims=True)
        eq = vals == m
        cnt = jnp.sum(jnp.where(eq, 1.0, 0.0), axis=0, keepdims=True)
        if want_rank:
            rank = jnp.minimum(rank, jnp.where(eq, seen, float(n)))
            seen = seen + cnt
        tops.append(m)
        cnts.append(cnt)
        vals = jnp.where(eq, -jnp.inf, vals)
    return jnp.concatenate(tops, axis=0), jnp.concatenate(cnts, axis=0), rank


def _pair_table(a, b, op):
    half = PEER_TOPK // 2
    pieces = [op(a[0:1], b)] + [op(a[r:r + 1], b[0:half]) for r in range(1, half)] + [op(a[half:], b[0:1])]
    return jnp.concatenate(pieces, axis=0)


def _route_kernel(q_ref, keys_ref, n1_ref, rk2_ref, p1_ref, p2_ref, s_sc):
    k = PEER_TOPK
    tt = q_ref.shape[0]
    for h in range(PEER_HEADS):
        s_sc[0] = _dot_nt(keys_ref[h, 0], q_ref[:, (2 * h) * LANES:(2 * h + 1) * LANES])
        s_sc[1] = _dot_nt(keys_ref[h, 1], q_ref[:, (2 * h + 1) * LANES:(2 * h + 2) * LANES])
        for lc in range(tt // LANES):
            cols = slice(lc * LANES, (lc + 1) * LANES)
            s1 = s_sc[0, :, cols]
            s2 = s_sc[1, :, cols]
            v1, c1, _ = _take_top(s1, k, False)
            v2, c2, rk2 = _take_top(s2, k, True)
            cand = _pair_table(v1, v2, jnp.add)
            mult = _pair_table(c1, c2, jnp.multiply)
            thr = jnp.full((1, LANES), jnp.inf, F32)
            seen = jnp.zeros((1, LANES), F32)
            rem = cand
            for _ in range(k):
                m = jnp.max(rem, axis=0, keepdims=True)
                eq = rem == m
                thr = jnp.where(seen < k, m, thr)
                seen = seen + jnp.sum(jnp.where(eq, mult, 0.0), axis=0, keepdims=True)
                rem = jnp.where(eq, -jnp.inf, rem)
            pair = _pair_table(jnp.exp(v1 - v1[0:1]), jnp.exp(v2 - v2[0:1]), jnp.multiply)
            z = jnp.sum(jnp.where(cand >= thr, mult * pair, 0.0), axis=0, keepdims=True)
            n1 = jnp.zeros_like(s1)
            for r in range(k):
                n1 = n1 + jnp.where(s1 + v2[r:r + 1] >= thr, c2[r:r + 1], 0.0)
            n1_ref[h, :, cols] = n1
            rk2_ref[h, :, cols] = rk2.astype(BF16)
            p1_ref[h, :, cols] = jnp.exp(s1 - v1[0:1])
            p2_ref[h, :, cols] = (jnp.exp(s2 - v2[0:1]) / z).astype(BF16)


def _peer_route(qp, keys):
    tp = qp.shape[0]
    tt = TOKEN_TILE
    big = pl.BlockSpec((PEER_HEADS, PEER_KEYS, tt), lambda j: (0, 0, j))
    wide = jax.ShapeDtypeStruct((PEER_HEADS, PEER_KEYS, tp), F32)
    narrow = jax.ShapeDtypeStruct((PEER_HEADS, PEER_KEYS, tp), BF16)
    return pl.pallas_call(
        _route_kernel,
        grid=(tp // tt,),
        in_specs=[
            pl.BlockSpec((tt, qp.shape[1]), lambda j: (j, 0)),
            pl.BlockSpec(keys.shape, lambda j: (0, 0, 0, 0)),
        ],
        out_specs=[big, big, big, big],
        out_shape=[wide, narrow, wide, narrow],
        scratch_shapes=[pltpu.VMEM((2, PEER_KEYS, tt), F32)],
        compiler_params=_cparams(("arbitrary",)),
        name="peer_route",
    )(qp, keys)


def _gelu(a):
    return 0.5 * a * (1.0 + lax.erf(a * (2.0 ** -0.5)))


def _peer_expert_kernel(x_ref, sc_ref, sh_ref, gate_ref, g_ref, b_ref, u_ref, v_ref,
                        n1_ref, rk2_ref, p1_ref, p2_ref, o_ref, ux_sc, w0_sc, w1_sc, at0_sc, at1_sc, acc_sc,
                        *, alpha, te, tt, n_e):
    i = pl.program_id(1)
    pack = BF16_SUBLANES
    d = acc_sc.shape[1]
    n_blk = te // PEER_KEYS

    @pl.when(i == 0)
    def _():
        ux_sc[...] = (x_ref[...] * (1.0 + sc_ref[0]) + sh_ref[0]).astype(BF16)
        acc_sc[...] = jnp.zeros_like(acc_sc)
        w1_sc[...] = jnp.zeros_like(w1_sc)

    zero = jnp.zeros((pack, LANES), BF16)

    at_bufs = (at0_sc, at1_sc)
    eg = at0_sc.shape[0] // PEER_KEYS
    tg = at0_sc.shape[1] // LANES
    n_eg, n_tg = n_blk // eg, tt // LANES // tg
    n_pieces = n_eg * n_tg
    dn = d // n_pieces

    def hidden(r):
        ge, gt = divmod(r, n_tg)
        at_bufs[r % 2][...] = _dot_nt(u_ref[ge * eg * PEER_KEYS:(ge + 1) * eg * PEER_KEYS, :],
                                      ux_sc[gt * tg * LANES:(gt + 1) * tg * LANES, :])

    def cell(cur_sc, r, ca, cl):
        ge, gt = divmod(r, n_tg)
        a, lc = ge * eg + ca, gt * tg + cl
        cols = slice(lc * LANES, (lc + 1) * LANES)
        a_t = at_bufs[r % 2].at[ca * PEER_KEYS:(ca + 1) * PEER_KEYS, cl * LANES:(cl + 1) * LANES]
        n1 = [jnp.broadcast_to(n1_ref[h, a:a + 1, cols], (pack, LANES)).astype(BF16) for h in range(PEER_HEADS)]
        p1 = [jnp.broadcast_to(p1_ref[h, a:a + 1, cols], (pack, LANES)).astype(BF16) for h in range(PEER_HEADS)]
        pieces = []
        for rg in range(PEER_KEYS // pack):
            rows = slice(rg * pack, (rg + 1) * pack)
            g = zero
            for h in range(PEER_HEADS):
                sel_p1 = jnp.minimum(jnp.maximum(n1[h] - rk2_ref[h, rows, cols], zero), p1[h])
                g = g + sel_p1 * p2_ref[h, rows, cols]
            pieces.append(g * _gelu(a_t[rows, :]).astype(BF16))
        blk = jnp.concatenate(pieces, axis=0)
        cur_sc[cols, a * PEER_KEYS:(a + 1) * PEER_KEYS] = blk.T

    def step(cur_sc, prev_sc):
        hidden(0)
        for r in range(n_pieces):
            cells = [(ca, cl) for ca in range(eg) for cl in range(tg)]
            half = len(cells) // 2
            if r + 1 < n_pieces:
                hidden(r + 1)
            for ca, cl in cells[:half]:
                cell(cur_sc, r, ca, cl)
            acc_sc[:, r * dn:(r + 1) * dn] += jnp.dot(prev_sc[...], v_ref[:, r * dn:(r + 1) * dn],
                                                      preferred_element_type=F32)
            for ca, cl in cells[half:]:
                cell(cur_sc, r, ca, cl)

    @pl.when((i < n_e) & (i % 2 == 0))
    def _():
        step(w0_sc, w1_sc)

    @pl.when((i < n_e) & (i % 2 == 1))
    def _():
        step(w1_sc, w0_sc)

    @pl.when(i == n_e)
    def _():
        last_sc = w1_sc if n_e % 2 == 0 else w0_sc
        y = acc_sc[...] + jnp.dot(last_sc[...], v_ref[...], preferred_element_type=F32)
        z = alpha * x_ref[...] + gate_ref[0] * y
        o_ref[...] = _layer_norm(z, g_ref[...], b_ref[...])


def _peer_experts(xs, modv, layer, n_lat_tiles, n_out_tiles, ln_g, ln_b, u, v, route, alpha):
    tp, d = xs.shape
    tt = TOKEN_TILE
    te = 8 * PEER_KEYS
    n_exp = u.shape[0]
    n1, rk2, p1, p2 = route
    once = pl.Buffered(1)
    tok = lambda j, i: (j, 0)
    fixed = lambda j, i: (0, 0)
    mod = lambda chunk: (lambda j, i: _mod_index(layer, chunk, n_lat_tiles)(j))
    rt = lambda: pl.BlockSpec((PEER_HEADS, PEER_KEYS, tt), lambda j, i: (0, 0, j), pipeline_mode=once)
    n_e = n_exp // te
    build = lambda i: jnp.minimum(i, n_e - 1)
    drain = lambda i: jnp.maximum(i - 1, 0)
    rt1 = lambda: pl.BlockSpec((PEER_HEADS, te // PEER_KEYS, tt), lambda j, i: (0, build(i), j))
    return pl.pallas_call(
        functools.partial(_peer_expert_kernel, alpha=alpha, te=te, tt=tt, n_e=n_e),
        grid=(n_out_tiles, n_e + 1),
        in_specs=[
            pl.BlockSpec((tt, d), tok, pipeline_mode=once),
            pl.BlockSpec((1, 1, d), mod(4)),
            pl.BlockSpec((1, 1, d), mod(3)),
            pl.BlockSpec((1, 1, d), mod(5)),
            pl.BlockSpec((1, d), fixed),
            pl.BlockSpec((1, d), fixed),
            pl.BlockSpec((te, d), lambda j, i: (build(i), 0)),
            pl.BlockSpec((te, d), lambda j, i: (drain(i), 0)),
            rt1(), rt(), rt1(), rt(),
        ],
        out_specs=pl.BlockSpec((tt, d), tok),
        out_shape=jax.ShapeDtypeStruct((n_out_tiles * tt, d), F32),
        scratch_shapes=[
            pltpu.VMEM((tt, d), BF16),
            pltpu.VMEM((tt, te), BF16),
            pltpu.VMEM((tt, te), BF16),
            pltpu.VMEM((2 * PEER_KEYS, 2 * LANES), F32),
            pltpu.VMEM((2 * PEER_KEYS, 2 * LANES), F32),
            pltpu.VMEM((tt, d), F32),
        ],
        compiler_params=_cparams(("arbitrary", "arbitrary")),
        name="peer_experts_ln",
    )(xs, modv, modv, modv, ln_g, ln_b, u, v, n1, rk2, p1, p2)


def _permute_qk_columns(w):
    d = w.shape[0]
    a_end, b_start, b_end = COL_AV * LANES, COL_BQ * LANES, COL_BV * LANES
    wa = w[:, :a_end].reshape(d, COL_AV, 2, A_QK_DIM // 2, 2)
    wa = wa.transpose(0, 1, 4, 2, 3).reshape(d, a_end)
    wb = w[:, b_start:b_end].reshape(d, COL_BV - COL_BQ, HEAD_DIM // 2, 2)
    wb = wb.transpose(0, 1, 3, 2).reshape(d, b_end - b_start)
    return jnp.concatenate([wa, w[:, a_end:b_start], wb, w[:, b_end:]], axis=1)


def _rope_tables(s_lat, tp):
    t = jnp.arange(s_lat)
    row = (t // GRID_W).astype(F32)
    col = (t % GRID_W).astype(F32)
    lane = np.arange(LANES)
    sign = jnp.asarray(np.where(lane < LANES // 2, -1.0, 1.0), F32)

    def tabs(dim, lane_to_pair):
        d_axis = dim // 2
        inv = ROPE_BASE ** (-jnp.arange(0, d_axis, 2, dtype=F32) / d_axis)
        ang = jnp.concatenate([row[:, None] * inv, col[:, None] * inv], axis=-1)
        ang = ang[:, lane_to_pair]
        pad = ((0, tp - s_lat), (0, 0))
        return (jnp.pad(jnp.cos(ang), pad, constant_values=1.0),
                jnp.pad(jnp.sin(ang) * sign, pad, constant_values=0.0))

    ca, sa = tabs(A_QK_DIM, lane % 32)
    cb, sb = tabs(HEAD_DIM, lane % 64)
    return jnp.stack([ca, cb]), jnp.stack([sa, sb])


def kernel(x, c, ctx, c_ctx, w_ada, b_ada, w_in, w_out, beta_out, ln1_g, ln1_b, ln2_g, ln2_b, diff_lq1, diff_lk1, diff_lq2, diff_lk2, diff_subln, sink, na_rpb, peer_wq, peer_keys, peer_u, peer_v):
    depth = w_ada.shape[0]
    _, s_lat, d = x.shape
    n_ctx = ctx.shape[1]
    assert x.shape[0] == 1 and s_lat % TOKEN_TILE == 0 and n_ctx == 256 and s_lat % n_ctx == 0
    n_lat_tiles = s_lat // TOKEN_TILE
    tp = (n_lat_tiles + 1) * TOKEN_TILE
    alpha = (2 * depth) ** 0.25

    xs = jnp.concatenate([x[0], ctx[0], jnp.zeros((tp - s_lat - n_ctx, d), x.dtype)], axis=0)
    cc = jnp.concatenate([c[0:1], c_ctx[None, :], jnp.zeros((6, d), c.dtype)], axis=0)
    mod = _ada_modulation(cc, w_ada, b_ada)
    modv = mod[:, :2, :].reshape(depth * 2 * 6, 1, d)
    rope_tabs = _rope_tables(s_lat, tp)
    na_tabs = _na_bias_tables(na_rpb.reshape((depth * C_HEADS,) + na_rpb.shape[2:]), s_lat)

    for l in range(depth):
        lam_init = 0.8 - 0.6 * math.exp(-0.3 * l)
        last = l == depth - 1
        w_in_l = _permute_qk_columns(w_in[l].astype(BF16))
        proj, vt = _projection(xs, modv, l, 1, 0, w_in_l, n_lat_tiles, rope_tabs)
        beta = beta_out[l][None, :]
        ma = _attention_a(proj, vt, diff_lq1[l][None], diff_lk1[l][None], diff_lq2[l][None], diff_lk2[l][None],
                          diff_subln[l][None], beta, s_lat, n_ctx, lam_init)
        mb = _attention_b(proj, sink[l], beta, s_lat, n_ctx)
        mc = _attention_c(proj, na_tabs, l * C_HEADS, beta, s_lat, n_ctx)
        xs = _out_projection(ma, mb, mc, w_out[l].astype(BF16), xs, modv, l, n_lat_tiles,
                             ln1_g[l][None], ln1_b[l][None], alpha)
        qp = _projection(xs, modv, l, 4, 3, peer_wq[l].astype(BF16), n_lat_tiles)
        route = _peer_route(qp, peer_keys[l].astype(BF16))
        xs = _peer_experts(xs, modv, l, n_lat_tiles, n_lat_tiles if last else n_lat_tiles + 1,
                           ln2_g[l][None], ln2_b[l][None], peer_u[l].astype(BF16), peer_v[l].astype(BF16),
                           route, alpha)
    return xs[None]
```

```python
import functools
import math

import numpy as np
import jax
import jax.numpy as jnp
from jax import lax
from jax.experimental import pallas as pl
from jax.experimental.pallas import tpu as pltpu

F32 = jnp.float32
BF16 = jnp.bfloat16

GRID_W = 64
HEAD_DIM = 128
A_HEADS = 4
B_HEADS = 8
B_KV_HEADS = 2
B_GROUP = 4
C_HEADS = 4
A_QK_DIM = 64
WINDOW = 128
NA_KH = 8
NA_KW = 16
NA_HALO_ROWS = NA_KH // 2
PEER_HEADS = 8
PEER_KEYS = 128
PEER_TOPK = 16
ROPE_BASE = 10000.0
LN_EPS = 1e-5

LANES = 128
BF16_SUBLANES = 16
TOKEN_TILE = 512
KEY_CHUNK = 256
A_ONES_ROWS = 16
A_Q_SCALE = A_QK_DIM ** -0.5 * math.log2(math.e)
VMEM_LIMIT = 56 * 1024 * 1024
CAST_BLOCK_BYTES = 4 * 1024 * 1024

COL_AQ, COL_AK, COL_AV = 0, 4, 8
COL_BQ, COL_BK, COL_BV = 12, 20, 22
COL_CQ, COL_CK, COL_CV = 24, 28, 32
N_COLBLOCKS = 36


def _cparams(sem):
    return pltpu.CompilerParams(dimension_semantics=sem, vmem_limit_bytes=VMEM_LIMIT)


def _dot_nt(a, b):
    return lax.dot_general(a, b, (((1,), (1,)), ((), ())), preferred_element_type=F32)


def _dot_tn(a, b):
    return lax.dot_general(a, b, (((0,), (0,)), ((), ())), preferred_element_type=F32)


def _layer_norm(y, g, b):
    mu = jnp.mean(y, axis=-1, keepdims=True)
    d = y - mu
    var = jnp.mean(d * d, axis=-1, keepdims=True)
    return d * lax.rsqrt(var + LN_EPS) * g + b


def _cast_kernel(x_ref, o_ref):
    o_ref[...] = x_ref[...].astype(o_ref.dtype)


def _to_bf16(w, layer):
    _, r, c = w.shape
    tr = CAST_BLOCK_BYTES // (4 * c) // 8 * 8
    while r % tr:
        tr -= 8
    return pl.pallas_call(
        _cast_kernel,
        grid=(r // tr,),
        in_specs=[pl.BlockSpec((None, tr, c), lambda i: (layer, i, 0))],
        out_specs=pl.BlockSpec((tr, c), lambda i: (i, 0)),
        out_shape=jax.ShapeDtypeStruct((r, c), BF16),
        compiler_params=_cparams(("arbitrary",)),
        name="weight_to_bf16",
    )(w)


def _ada_kernel(c_ref, w_ref, b_ref, o_ref):
    c = c_ref[...]
    a = c * jax.nn.sigmoid(c)
    o_ref[0] = jnp.dot(a.astype(BF16), w_ref[0].astype(BF16), preferred_element_type=F32) + b_ref[0]


def _ada_modulation(cc, w_ada, b_ada):
    depth, d, n = w_ada.shape
    tn = 1024
    return pl.pallas_call(
        _ada_kernel,
        grid=(depth, n // tn),
        in_specs=[
            pl.BlockSpec((8, d), lambda l, j: (0, 0)),
            pl.BlockSpec((1, d, tn), lambda l, j: (l, 0, j)),
            pl.BlockSpec((1, 1, tn), lambda l, j: (l, 0, j)),
        ],
        out_specs=pl.BlockSpec((1, 8, tn), lambda l, j: (l, 0, j)),
        out_shape=jax.ShapeDtypeStruct((depth, 8, n), F32),
        compiler_params=_cparams(("arbitrary", "arbitrary")),
        name="ada_modulation",
    )(cc, w_ada, b_ada.reshape(depth, 1, n))


def _rope_kind(colblock):
    if colblock < COL_AV:
        return 0
    if COL_BQ <= colblock < COL_BV:
        return 1
    return None


def _proj_kernel(x_ref, sc_ref, sh_ref, w_ref, *rest, tn, rope):
    if rope:
        cos_ref, sin_ref, o_ref, vt_ref = rest
    else:
        (o_ref,) = rest
    xm = (x_ref[...] * (1.0 + sc_ref[0]) + sh_ref[0]).astype(BF16)
    n = w_ref.shape[1]
    for jn in range(n // tn):
        y = jnp.dot(xm, w_ref[:, jn * tn:(jn + 1) * tn], preferred_element_type=F32)
        for jb in range(tn // LANES):
            colblock = jn * (tn // LANES) + jb
            yb = y[:, jb * LANES:(jb + 1) * LANES]
            kind = _rope_kind(colblock) if rope else None
            if kind is not None:
                yb = yb * cos_ref[kind] + pltpu.roll(yb, LANES // 2, 1) * sin_ref[kind]
            if rope and colblock < COL_AK:
                yb = yb * A_Q_SCALE
            o_ref[:, colblock * LANES:(colblock + 1) * LANES] = yb.astype(o_ref.dtype)
            if rope and COL_AV <= colblock < COL_BQ:
                hd = colblock - COL_AV
                for ck in range(yb.shape[0] // KEY_CHUNK):
                    vt_ref[ck, hd * HEAD_DIM:(hd + 1) * HEAD_DIM, :] = (
                        yb[ck * KEY_CHUNK:(ck + 1) * KEY_CHUNK, :].T.astype(vt_ref.dtype))


def _mod_index(layer_slot, chunk, n_lat_tiles):
    def index(i):
        who = jnp.where(i >= n_lat_tiles, 1, 0)
        return ((layer_slot * 2 + who) * 6 + chunk, 0, 0)
    return index


def _projection(xs, modv, layer, chunk_scale, chunk_shift, w, n_lat_tiles, rope_tabs=None):
    tp, d = xs.shape
    n = w.shape[1]
    tm = TOKEN_TILE
    rope = rope_tabs is not None
    in_specs = [
        pl.BlockSpec((tm, d), lambda i: (i, 0)),
        pl.BlockSpec((1, 1, d), _mod_index(layer, chunk_scale, n_lat_tiles)),
        pl.BlockSpec((1, 1, d), _mod_index(layer, chunk_shift, n_lat_tiles)),
        pl.BlockSpec((d, n), lambda i: (0, 0), pipeline_mode=pl.Buffered(1)),
    ]
    args = [xs, modv, modv, w]
    if rope:
        in_specs += [pl.BlockSpec((2, tm, LANES), lambda i: (0, i, 0))] * 2
        args += list(rope_tabs)
    out_specs = pl.BlockSpec((tm, n), lambda i: (i, 0))
    out_shape = jax.ShapeDtypeStruct((tp, n), BF16)
    if rope:
        per = tm // KEY_CHUNK
        out_specs = [out_specs, pl.BlockSpec((per, A_HEADS * HEAD_DIM, KEY_CHUNK), lambda i: (i, 0, 0))]
        out_shape = [out_shape, jax.ShapeDtypeStruct((tp // KEY_CHUNK, A_HEADS * HEAD_DIM, KEY_CHUNK), BF16)]
    return pl.pallas_call(
        functools.partial(_proj_kernel, tn=512, rope=rope),
        grid=(tp // tm,),
        in_specs=in_specs,
        out_specs=out_specs,
        out_shape=out_shape,
        compiler_params=_cparams(("arbitrary",)),
        name="mod_projection_rope" if rope else "mod_projection",
    )(*args)


def _attn_a_kernel(lq1_ref, lk1_ref, lq2_ref, lk2_ref, subln_ref, beta_ref, q_ref, k_ref, vt_ref, o_ref,
                   qs_sc, s0_sc, s1_sc, p0_sc, p1_sc, m_sc, acc_sc, *, s_lat, tq, n_sub, n_lat_tiles, lam_init):
    i = pl.program_id(1)
    lane = lax.broadcasted_iota(jnp.int32, (1, LANES), 1)
    map1 = ((lane // 32) % 2) == 1
    q = q_ref[...]
    zero = jnp.zeros_like(q)
    qs_sc[0:tq, :] = jnp.where(map1, zero, q)
    qs_sc[tq:2 * tq, :] = jnp.where(map1, q, zero)
    m_sc[...] = jnp.full_like(m_sc, -jnp.inf)
    acc_sc[...] = jnp.zeros_like(acc_sc)
    ones_rows = jnp.where(lax.broadcasted_iota(jnp.int32, (A_ONES_ROWS, KEY_CHUNK), 0) == 0, 1.0, 0.0).astype(BF16)

    def keys(c):
        start = c * KEY_CHUNK
        if not isinstance(c, int):
            start = pl.multiple_of(start, KEY_CHUNK)
        return k_ref[pl.ds(start, KEY_CHUNK), :]

    def scores(group, s_sc):
        for j in range(n_sub):
            s_sc[j] = _dot_nt(keys(group * n_sub + j), qs_sc[...])

    def update(chunks, s):
        m_prev = m_sc[...]
        m_new = m_prev
        for sj in s:
            m_new = jnp.maximum(m_new, jnp.max(sj, axis=0, keepdims=True))
        acc = jnp.exp2(m_prev - m_new) * acc_sc[...]
        for c, sj in zip(chunks, s):
            p = jnp.exp2(sj - m_new).astype(BF16)
            v1 = jnp.concatenate([vt_ref[c], ones_rows], axis=0)
            acc = acc + jnp.dot(v1, p, preferred_element_type=F32)
        acc_sc[...] = acc
        m_sc[...] = m_new

    ctx_chunk = s_lat // KEY_CHUNK

    def context():
        update([ctx_chunk], [_dot_nt(keys(ctx_chunk), qs_sc[...])])

    @pl.when(i >= n_lat_tiles)
    def _():
        context()

    def values(group, p_sc):
        out = None
        for j in range(n_sub):
            v1 = jnp.concatenate([vt_ref[group * n_sub + j], ones_rows], axis=0)
            part = jnp.dot(v1, p_sc[j], preferred_element_type=F32)
            out = part if out is None else out + part
        return out

    def stage(g_scores, s_next_sc, g_values, p_prev_sc, s_cur_sc, p_cur_sc):
        acc_sc[...] += values(g_values, p_prev_sc)
        scores(g_scores, s_next_sc)
        m_prev = m_sc[...]
        m_new = m_prev
        for j in range(n_sub):
            m_new = jnp.maximum(m_new, jnp.max(s_cur_sc[j], axis=0, keepdims=True))
        for j in range(n_sub):
            p_cur_sc[j] = jnp.exp2(s_cur_sc[j] - m_new).astype(BF16)
        acc_sc[...] *= jnp.exp2(m_prev - m_new)
        m_sc[...] = m_new

    @pl.when(i < n_lat_tiles)
    def _():
        n_groups = s_lat // (KEY_CHUNK * n_sub)
        scores(0, s0_sc)
        context()
        p1_sc[...] = jnp.zeros_like(p1_sc)

        unroll = math.gcd(n_groups, 8)
        assert unroll % 2 == 0

        def body(gu, carry):
            for u in range(unroll):
                g = unroll * gu + u
                bufs = (s1_sc, p1_sc, s0_sc, p0_sc) if u % 2 == 0 else (s0_sc, p0_sc, s1_sc, p1_sc)
                s_next, p_prev, s_cur, p_cur = bufs
                stage(jnp.minimum(g + 1, n_groups - 1), s_next, jnp.maximum(g - 1, 0), p_prev, s_cur, p_cur)
            return carry
        lax.fori_loop(0, n_groups // unroll, body, 0)
        acc_sc[...] += values(n_groups - 1, p1_sc)

    lam = (jnp.exp(jnp.sum(lq1_ref[...] * lk1_ref[...], axis=-1, keepdims=True))
           - jnp.exp(jnp.sum(lq2_ref[...] * lk2_ref[...], axis=-1, keepdims=True)) + lam_init)
    l = acc_sc[HEAD_DIM:HEAD_DIM + 1, :]
    o_t = (acc_sc[0:HEAD_DIM, 0:tq] / l[:, 0:tq] - lam * (acc_sc[0:HEAD_DIM, tq:2 * tq] / l[:, tq:2 * tq]))
    o = o_t.T
    y = o * lax.rsqrt(jnp.mean(o * o, axis=-1, keepdims=True) + LN_EPS) * subln_ref[...]
    o_ref[...] = (y * (1.0 - lam_init) * beta_ref[...]).astype(o_ref.dtype)


def _attention_a(proj, vt, lq1, lk1, lq2, lk2, subln, beta, s_lat, n_ctx, lam_init):
    tp = proj.shape[0]
    assert n_ctx == KEY_CHUNK
    tq, n_sub = 256, 2
    n_lat_tiles = s_lat // tq
    vec = lambda: pl.BlockSpec((1, A_QK_DIM), lambda h, i: (0, 0))
    return pl.pallas_call(
        functools.partial(_attn_a_kernel, s_lat=s_lat, tq=tq, n_sub=n_sub, n_lat_tiles=n_lat_tiles,
                          lam_init=lam_init),
        grid=(A_HEADS, tp // tq),
        in_specs=[
            vec(), vec(), vec(), vec(),
            pl.BlockSpec((1, HEAD_DIM), lambda h, i: (0, 0)),
            pl.BlockSpec((1, HEAD_DIM), lambda h, i: (0, h)),
            pl.BlockSpec((tq, HEAD_DIM), lambda h, i: (i, COL_AQ + h)),
            pl.BlockSpec((tp, HEAD_DIM), lambda h, i: (0, COL_AK + h)),
            pl.BlockSpec((tp // KEY_CHUNK, HEAD_DIM, KEY_CHUNK), lambda h, i: (0, h, 0)),
        ],
        out_specs=pl.BlockSpec((tq, HEAD_DIM), lambda h, i: (i, h)),
        out_shape=jax.ShapeDtypeStruct((tp, A_HEADS * HEAD_DIM), BF16),
        scratch_shapes=[
            pltpu.VMEM((2 * tq, HEAD_DIM), BF16),
            pltpu.VMEM((n_sub, KEY_CHUNK, 2 * tq), F32),
            pltpu.VMEM((n_sub, KEY_CHUNK, 2 * tq), F32),
            pltpu.VMEM((n_sub, KEY_CHUNK, 2 * tq), BF16),
            pltpu.VMEM((n_sub, KEY_CHUNK, 2 * tq), BF16),
            pltpu.VMEM((1, 2 * tq), F32),
            pltpu.VMEM((HEAD_DIM + A_ONES_ROWS, 2 * tq), F32),
        ],
        compiler_params=_cparams(("arbitrary", "arbitrary")),
        name="attn_diff",
    )(lq1, lk1, lq2, lk2, subln, beta, proj, proj, vt)


def _attn_b_kernel(sink_ref, beta_ref, q_ref, kp_ref, kc_ref, kn_ref, kx_ref, vp_ref, vc_ref, vn_ref, vx_ref,
                   o_ref, *, s_lat, tq):
    kv = pl.program_id(0)
    i = pl.program_id(1)
    scale = HEAD_DIM ** -0.5
    q0 = i * tq
    qpos = q0 + lax.broadcasted_iota(jnp.int32, (tq, 1), 0)

    def valid(kstart, n):
        kpos = kstart + lax.broadcasted_iota(jnp.int32, (1, n), 1)
        return (jnp.abs(kpos - qpos) <= WINDOW) & (kpos >= 0) & (kpos < s_lat) & (qpos < s_lat)

    ok_p = valid(q0 - WINDOW, WINDOW)
    ok_c = valid(q0, tq)
    ok_n = valid(q0 + tq, WINDOW)
    neg = -jnp.inf
    for g in range(B_GROUP):
        qg = q_ref[:, g * HEAD_DIM:(g + 1) * HEAD_DIM]
        s_x = _dot_nt(qg, kx_ref[...]) * scale
        s_p = jnp.where(ok_p, _dot_nt(qg, kp_ref[...]) * scale, neg)
        s_c = jnp.where(ok_c, _dot_nt(qg, kc_ref[...]) * scale, neg)
        s_n = jnp.where(ok_n, _dot_nt(qg, kn_ref[...]) * scale, neg)
        snk = sink_ref[kv * B_GROUP + g]
        mx = jnp.maximum(jnp.maximum(jnp.max(s_x, -1, keepdims=True), jnp.max(s_p, -1, keepdims=True)),
                         jnp.maximum(jnp.max(s_c, -1, keepdims=True), jnp.max(s_n, -1, keepdims=True)))
        mx = jnp.maximum(mx, snk)
        e_x = jnp.exp(s_x - mx)
        e_p = jnp.exp(s_p - mx)
        e_c = jnp.exp(s_c - mx)
        e_n = jnp.exp(s_n - mx)
        den = (jnp.exp(snk - mx) + jnp.sum(e_x, -1, keepdims=True) + jnp.sum(e_p, -1, keepdims=True)
               + jnp.sum(e_c, -1, keepdims=True) + jnp.sum(e_n, -1, keepdims=True))
        o = (jnp.dot(e_x.astype(BF16), vx_ref[...], preferred_element_type=F32)
             + jnp.dot(e_p.astype(BF16), vp_ref[...], preferred_element_type=F32)
             + jnp.dot(e_c.astype(BF16), vc_ref[...], preferred_element_type=F32)
             + jnp.dot(e_n.astype(BF16), vn_ref[...], preferred_element_type=F32))
        o_ref[:, g * HEAD_DIM:(g + 1) * HEAD_DIM] = (
            (o / den) * beta_ref[:, g * HEAD_DIM:(g + 1) * HEAD_DIM]).astype(o_ref.dtype)


def _attention_b(proj, sink, beta, s_lat, n_ctx):
    tp = proj.shape[0]
    tq = 256
    per = tq // WINDOW
    last_halo = tp // WINDOW - 1
    gw = B_GROUP * HEAD_DIM

    def halo_prev(col):
        return pl.BlockSpec((WINDOW, HEAD_DIM), lambda kv, i: (jnp.maximum(i * per - 1, 0), col + kv))

    def halo_next(col):
        return pl.BlockSpec((WINDOW, HEAD_DIM), lambda kv, i: (jnp.minimum((i + 1) * per, last_halo), col + kv))

    def cur(col):
        return pl.BlockSpec((tq, HEAD_DIM), lambda kv, i: (i, col + kv))

    def ctx(col):
        return pl.BlockSpec((n_ctx, HEAD_DIM), lambda kv, i: (s_lat // n_ctx, col + kv))

    return pl.pallas_call(
        functools.partial(_attn_b_kernel, s_lat=s_lat, tq=tq),
        grid=(B_KV_HEADS, tp // tq),
        in_specs=[
            pl.BlockSpec(memory_space=pltpu.SMEM),
            pl.BlockSpec((1, gw), lambda kv, i: (0, (A_HEADS * HEAD_DIM) // gw + kv)),
            pl.BlockSpec((tq, gw), lambda kv, i: (i, (COL_BQ * LANES) // gw + kv)),
            halo_prev(COL_BK), cur(COL_BK), halo_next(COL_BK), ctx(COL_BK),
            halo_prev(COL_BV), cur(COL_BV), halo_next(COL_BV), ctx(COL_BV),
        ],
        out_specs=pl.BlockSpec((tq, gw), lambda kv, i: (i, kv)),
        out_shape=jax.ShapeDtypeStruct((tp, B_HEADS * HEAD_DIM), BF16),
        compiler_params=_cparams(("arbitrary", "arbitrary")),
        name="attn_window",
    )(sink, beta, proj, proj, proj, proj, proj, proj, proj, proj, proj)


def _na_bias_tables(rpb, s_lat):
    rows = s_lat // GRID_W
    tr = TOKEN_TILE // GRID_W
    kh = min(NA_KH, rows)

    neg = -1e30
    n_heads = rpb.shape[0]
    per_c = []
    for c in range(GRID_W):
        c0 = min(max(c - NA_KW // 2, 0), GRID_W - NA_KW)
        lo = c0 - c + NA_KW - 1
        per_c.append(jnp.pad(rpb[:, :, lo:lo + NA_KW], ((0, 0), (0, 0), (c0, GRID_W - NA_KW - c0)),
                             constant_values=neg))
    colb = jnp.stack(per_c, axis=2).astype(F32)
    off = jnp.full((n_heads, GRID_W, GRID_W), neg, F32)

    hr = NA_HALO_ROWS

    def table(q_row0, n_rows):
        per_r = []
        for r in range(q_row0, q_row0 + tr):
            r0 = min(max(r - kh // 2, 0), n_rows - kh)
            assert q_row0 - hr <= r0 and r0 + kh <= q_row0 + tr + hr
            blocks = []
            for kr in range(q_row0 - hr, q_row0 + tr + hr):
                inside = r0 <= kr < r0 + kh and 0 <= kr < n_rows
                blocks.append(colb[:, kr - r + NA_KH - 1] if inside else off)
            per_r.append(jnp.concatenate(blocks, axis=-1))
        return jnp.stack(per_r, axis=1).reshape(n_heads, TOKEN_TILE, TOKEN_TILE + 2 * hr * GRID_W)

    far = 4 * tr
    out = [table(0, rows), table(far, 2 * far + tr), table(rows - tr, rows)]
    out.append(jnp.full_like(out[0], neg))
    return jnp.stack(out, axis=0)


def _attn_c_kernel(beta_ref, bias_ref, q_ref, kp_ref, kc_ref, kn_ref, kx_ref, vp_ref, vc_ref, vn_ref, vx_ref,
                   o_ref, *, tq):
    scale = HEAD_DIM ** -0.5
    q = q_ref[...]
    hq = kp_ref.shape[0]
    s_x = _dot_nt(q, kx_ref[...]) * scale
    s_p = _dot_nt(q, kp_ref[...]) * scale + bias_ref[0, 0, :, 0:hq]
    s_c = _dot_nt(q, kc_ref[...]) * scale + bias_ref[0, 0, :, hq:hq + tq]
    s_n = _dot_nt(q, kn_ref[...]) * scale + bias_ref[0, 0, :, hq + tq:2 * hq + tq]
    mx = jnp.maximum(jnp.maximum(jnp.max(s_x, -1, keepdims=True), jnp.max(s_p, -1, keepdims=True)),
                     jnp.maximum(jnp.max(s_c, -1, keepdims=True), jnp.max(s_n, -1, keepdims=True)))
    e_x = jnp.exp(s_x - mx)
    e_p = jnp.exp(s_p - mx)
    e_c = jnp.exp(s_c - mx)
    e_n = jnp.exp(s_n - mx)
    den = (jnp.sum(e_x, -1, keepdims=True) + jnp.sum(e_p, -1, keepdims=True)
           + jnp.sum(e_c, -1, keepdims=True) + jnp.sum(e_n, -1, keepdims=True))
    o = (jnp.dot(e_x.astype(BF16), vx_ref[...], preferred_element_type=F32)
         + jnp.dot(e_p.astype(BF16), vp_ref[...], preferred_element_type=F32)
         + jnp.dot(e_c.astype(BF16), vc_ref[...], preferred_element_type=F32)
         + jnp.dot(e_n.astype(BF16), vn_ref[...], preferred_element_type=F32))
    o_ref[...] = ((o / den) * beta_ref[...]).astype(o_ref.dtype)


def _attention_c(proj, bias_tabs, bias_head0, beta, s_lat, n_ctx):
    tp = proj.shape[0]
    tq = TOKEN_TILE
    hq = NA_HALO_ROWS * GRID_W
    per = tq // hq
    n_lat_tiles = s_lat // tq
    n_tiles = tp // tq
    beta_col0 = (A_HEADS + B_HEADS)

    def kind(i):
        return jnp.where(i >= n_lat_tiles, 3, jnp.where(i == 0, 0, jnp.where(i == n_lat_tiles - 1, 2, 1)))

    def prev(col):
        return pl.BlockSpec((hq, HEAD_DIM), lambda h, i: (jnp.maximum(i * per - 1, 0), col + h))

    def cur(col):
        return pl.BlockSpec((tq, HEAD_DIM), lambda h, i: (i, col + h))

    def nxt(col):
        return pl.BlockSpec((hq, HEAD_DIM), lambda h, i: (jnp.minimum((i + 1) * per, tp // hq - 1), col + h))

    def ctx(col):
        return pl.BlockSpec((n_ctx, HEAD_DIM), lambda h, i: (s_lat // n_ctx, col + h))

    return pl.pallas_call(
        functools.partial(_attn_c_kernel, tq=tq),
        grid=(C_HEADS, n_tiles),
        in_specs=[
            pl.BlockSpec((1, HEAD_DIM), lambda h, i: (0, beta_col0 + h)),
            pl.BlockSpec((1, 1, tq, tq + 2 * hq), lambda h, i: (kind(i), bias_head0 + h, 0, 0)),
            cur(COL_CQ),
            prev(COL_CK), cur(COL_CK), nxt(COL_CK), ctx(COL_CK),
            prev(COL_CV), cur(COL_CV), nxt(COL_CV), ctx(COL_CV),
        ],
        out_specs=pl.BlockSpec((tq, HEAD_DIM), lambda h, i: (i, h)),
        out_shape=jax.ShapeDtypeStruct((tp, C_HEADS * HEAD_DIM), BF16),
        compiler_params=_cparams(("arbitrary", "arbitrary")),
        name="attn_neighbourhood",
    )(beta, bias_tabs, proj, proj, proj, proj, proj, proj, proj, proj, proj)


def _outproj_kernel(ma_ref, mb_ref, mc_ref, w_ref, x_ref, gate_ref, g_ref, b_ref, o_ref, *, alpha):
    wa = A_HEADS * HEAD_DIM
    wb = wa + B_HEADS * HEAD_DIM
    y = (jnp.dot(ma_ref[...], w_ref[0:wa, :], preferred_element_type=F32)
         + jnp.dot(mb_ref[...], w_ref[wa:wb, :], preferred_element_type=F32)
         + jnp.dot(mc_ref[...], w_ref[wb:, :], preferred_element_type=F32))
    z = alpha * x_ref[...] + gate_ref[0] * y
    o_ref[...] = _layer_norm(z, g_ref[...], b_ref[...])


def _out_projection(ma, mb, mc, w_out, xs, modv, layer, n_lat_tiles, ln_g, ln_b, alpha):
    tp, d = xs.shape
    tm = TOKEN_TILE // 2
    n_lat = n_lat_tiles * (TOKEN_TILE // tm)
    row = lambda i: (i, 0)
    fixed = lambda i: (0, 0)
    return pl.pallas_call(
        functools.partial(_outproj_kernel, alpha=alpha),
        grid=(tp // tm,),
        in_specs=[
            pl.BlockSpec((tm, ma.shape[1]), row),
            pl.BlockSpec((tm, mb.shape[1]), row),
            pl.BlockSpec((tm, mc.shape[1]), row),
            pl.BlockSpec(w_out.shape, fixed, pipeline_mode=pl.Buffered(1)),
            pl.BlockSpec((tm, d), row),
            pl.BlockSpec((1, 1, d), _mod_index(layer, 2, n_lat)),
            pl.BlockSpec((1, d), fixed),
            pl.BlockSpec((1, d), fixed),
        ],
        out_specs=pl.BlockSpec((tm, d), row),
        out_shape=jax.ShapeDtypeStruct((tp, d), F32),
        compiler_params=_cparams(("arbitrary",)),
        name="out_projection_ln",
    )(ma, mb, mc, w_out, xs, modv, ln_g, ln_b)


def _take_top(vals, rounds, want_rank):
    tops, cnts = [], []
    n, t = vals.shape
    rank = jnp.full((n, t), float(n), F32) if want_rank else None
    seen = jnp.zeros((1, t), F32)
    for _ in range(rounds):
        m = jnp.max(vals, axis=0, keepdims=True)
        eq = vals == m
        cnt = jnp.sum(jnp.where(eq, 1.0, 0.0), axis=0, keepdims=True)
        if want_rank:
            rank = jnp.minimum(rank, jnp.where(eq, seen, float(n)))
            seen = seen + cnt
        tops.append(m)
        cnts.append(cnt)
        vals = jnp.where(eq, -jnp.inf, vals)
    return jnp.concatenate(tops, axis=0), jnp.concatenate(cnts, axis=0), rank


def _pair_table(a, b, op):
    half = PEER_TOPK // 2
    pieces = [op(a[0:1], b)] + [op(a[r:r + 1], b[0:half]) for r in range(1, half)] + [op(a[half:], b[0:1])]
    return jnp.concatenate(pieces, axis=0)


def _route_kernel(q_ref, keys_ref, n1_ref, rk2_ref, p1_ref, p2_ref, s_sc):
    k = PEER_TOPK
    tt = q_ref.shape[0]
    for h in range(PEER_HEADS):
        s_sc[0] = _dot_nt(keys_ref[h, 0], q_ref[:, (2 * h) * LANES:(2 * h + 1) * LANES])
        s_sc[1] = _dot_nt(keys_ref[h, 1], q_ref[:, (2 * h + 1) * LANES:(2 * h + 2) * LANES])
        for lc in range(tt // LANES):
            cols = slice(lc * LANES, (lc + 1) * LANES)
            s1 = s_sc[0, :, cols]
            s2 = s_sc[1, :, cols]
            v1, c1, _ = _take_top(s1, k, False)
            v2, c2, rk2 = _take_top(s2, k, True)
            cand = _pair_table(v1, v2, jnp.add)
            mult = _pair_table(c1, c2, jnp.multiply)
            thr = jnp.full((1, LANES), jnp.inf, F32)
            seen = jnp.zeros((1, LANES), F32)
            rem = cand
            for _ in range(k):
                m = jnp.max(rem, axis=0, keepdims=True)
                eq = rem == m
                thr = jnp.where(seen < k, m, thr)
                seen = seen + jnp.sum(jnp.where(eq, mult, 0.0), axis=0, keepdims=True)
                rem = jnp.where(eq, -jnp.inf, rem)
            pair = _pair_table(jnp.exp(v1 - v1[0:1]), jnp.exp(v2 - v2[0:1]), jnp.multiply)
            z = jnp.sum(jnp.where(cand >= thr, mult * pair, 0.0), axis=0, keepdims=True)
            n1 = jnp.zeros_like(s1)
            for r in range(k):
                n1 = n1 + jnp.where(s1 + v2[r:r + 1] >= thr, c2[r:r + 1], 0.0)
            n1_ref[h, :, cols] = n1
            rk2_ref[h, :, cols] = rk2.astype(BF16)
            p1_ref[h, :, cols] = jnp.exp(s1 - v1[0:1])
            p2_ref[h, :, cols] = (jnp.exp(s2 - v2[0:1]) / z).astype(BF16)


def _peer_route(qp, keys):
    tp = qp.shape[0]
    tt = TOKEN_TILE
    big = pl.BlockSpec((PEER_HEADS, PEER_KEYS, tt), lambda j: (0, 0, j))
    wide = jax.ShapeDtypeStruct((PEER_HEADS, PEER_KEYS, tp), F32)
    narrow = jax.ShapeDtypeStruct((PEER_HEADS, PEER_KEYS, tp), BF16)
    return pl.pallas_call(
        _route_kernel,
        grid=(tp // tt,),
        in_specs=[
            pl.BlockSpec((tt, qp.shape[1]), lambda j: (j, 0)),
            pl.BlockSpec(keys.shape, lambda j: (0, 0, 0, 0)),
        ],
        out_specs=[big, big, big, big],
        out_shape=[wide, narrow, wide, narrow],
        scratch_shapes=[pltpu.VMEM((2, PEER_KEYS, tt), F32)],
        compiler_params=_cparams(("arbitrary",)),
        name="peer_route",
    )(qp, keys)


def _gelu(a):
    return 0.5 * a * (1.0 + lax.erf(a * (2.0 ** -0.5)))


def _peer_expert_kernel(x_ref, sc_ref, sh_ref, gate_ref, g_ref, b_ref, u_ref, v_ref,
                        n1_ref, rk2_ref, p1_ref, p2_ref, o_ref, ux_sc, w0_sc, w1_sc, at0_sc, at1_sc, acc_sc,
                        *, alpha, te, tt, n_e):
    i = pl.program_id(1)
    pack = BF16_SUBLANES
    d = acc_sc.shape[1]
    n_blk = te // PEER_KEYS

    @pl.when(i == 0)
    def _():
        ux_sc[...] = (x_ref[...] * (1.0 + sc_ref[0]) + sh_ref[0]).astype(BF16)
        acc_sc[...] = jnp.zeros_like(acc_sc)
        w1_sc[...] = jnp.zeros_like(w1_sc)

    zero = jnp.zeros((pack, LANES), BF16)

    at_bufs = (at0_sc, at1_sc)
    eg = at0_sc.shape[0] // PEER_KEYS
    tg = at0_sc.shape[1] // LANES
    n_eg, n_tg = n_blk // eg, tt // LANES // tg
    n_pieces = n_eg * n_tg
    dn = d // n_pieces

    def hidden(r):
        ge, gt = divmod(r, n_tg)
        at_bufs[r % 2][...] = _dot_nt(u_ref[ge * eg * PEER_KEYS:(ge + 1) * eg * PEER_KEYS, :],
                                      ux_sc[gt * tg * LANES:(gt + 1) * tg * LANES, :])

    def cell(cur_sc, r, ca, cl):
        ge, gt = divmod(r, n_tg)
        a, lc = ge * eg + ca, gt * tg + cl
        cols = slice(lc * LANES, (lc + 1) * LANES)
        a_t = at_bufs[r % 2].at[ca * PEER_KEYS:(ca + 1) * PEER_KEYS, cl * LANES:(cl + 1) * LANES]
        n1 = [jnp.broadcast_to(n1_ref[h, a:a + 1, cols], (pack, LANES)).astype(BF16) for h in range(PEER_HEADS)]
        p1 = [jnp.broadcast_to(p1_ref[h, a:a + 1, cols], (pack, LANES)).astype(BF16) for h in range(PEER_HEADS)]
        pieces = []
        for rg in range(PEER_KEYS // pack):
            rows = slice(rg * pack, (rg + 1) * pack)
            g = zero
            for h in range(PEER_HEADS):
                sel_p1 = jnp.minimum(jnp.maximum(n1[h] - rk2_ref[h, rows, cols], zero), p1[h])
                g = g + sel_p1 * p2_ref[h, rows, cols]
            pieces.append(g * _gelu(a_t[rows, :]).astype(BF16))
        blk = jnp.concatenate(pieces, axis=0)
        cur_sc[cols, a * PEER_KEYS:(a + 1) * PEER_KEYS] = blk.T

    def step(cur_sc, prev_sc):
        hidden(0)
        for r in range(n_pieces):
            cells = [(ca, cl) for ca in range(eg) for cl in range(tg)]
            half = len(cells) // 2
            if r + 1 < n_pieces:
                hidden(r + 1)
            for ca, cl in cells[:half]:
                cell(cur_sc, r, ca, cl)
            acc_sc[:, r * dn:(r + 1) * dn] += jnp.dot(prev_sc[...], v_ref[:, r * dn:(r + 1) * dn],
                                                      preferred_element_type=F32)
            for ca, cl in cells[half:]:
                cell(cur_sc, r, ca, cl)

    @pl.when((i < n_e) & (i % 2 == 0))
    def _():
        step(w0_sc, w1_sc)

    @pl.when((i < n_e) & (i % 2 == 1))
    def _():
        step(w1_sc, w0_sc)

    @pl.when(i == n_e)
    def _():
        last_sc = w1_sc if n_e % 2 == 0 else w0_sc
        y = acc_sc[...] + jnp.dot(last_sc[...], v_ref[...], preferred_element_type=F32)
        z = alpha * x_ref[...] + gate_ref[0] * y
        o_ref[...] = _layer_norm(z, g_ref[...], b_ref[...])


def _peer_experts(xs, modv, layer, n_lat_tiles, n_out_tiles, ln_g, ln_b, u, v, route, alpha):
    tp, d = xs.shape
    tt = TOKEN_TILE
    te = 8 * PEER_KEYS
    n_exp = u.shape[0]
    n1, rk2, p1, p2 = route
    once = pl.Buffered(1)
    tok = lambda j, i: (j, 0)
    fixed = lambda j, i: (0, 0)
    mod = lambda chunk: (lambda j, i: _mod_index(layer, chunk, n_lat_tiles)(j))
    rt = lambda: pl.BlockSpec((PEER_HEADS, PEER_KEYS, tt), lambda j, i: (0, 0, j), pipeline_mode=once)
    n_e = n_exp // te
    build = lambda i: jnp.minimum(i, n_e - 1)
    drain = lambda i: jnp.maximum(i - 1, 0)
    rt1 = lambda: pl.BlockSpec((PEER_HEADS, te // PEER_KEYS, tt), lambda j, i: (0, build(i), j))
    return pl.pallas_call(
        functools.partial(_peer_expert_kernel, alpha=alpha, te=te, tt=tt, n_e=n_e),
        grid=(n_out_tiles, n_e + 1),
        in_specs=[
            pl.BlockSpec((tt, d), tok, pipeline_mode=once),
            pl.BlockSpec((1, 1, d), mod(4)),
            pl.BlockSpec((1, 1, d), mod(3)),
            pl.BlockSpec((1, 1, d), mod(5)),
            pl.BlockSpec((1, d), fixed),
            pl.BlockSpec((1, d), fixed),
            pl.BlockSpec((te, d), lambda j, i: (build(i), 0)),
            pl.BlockSpec((te, d), lambda j, i: (drain(i), 0)),
            rt1(), rt(), rt1(), rt(),
        ],
        out_specs=pl.BlockSpec((tt, d), tok),
        out_shape=jax.ShapeDtypeStruct((n_out_tiles * tt, d), F32),
        scratch_shapes=[
            pltpu.VMEM((tt, d), BF16),
            pltpu.VMEM((tt, te), BF16),
            pltpu.VMEM((tt, te), BF16),
            pltpu.VMEM((2 * PEER_KEYS, 2 * LANES), F32),
            pltpu.VMEM((2 * PEER_KEYS, 2 * LANES), F32),
            pltpu.VMEM((tt, d), F32),
        ],
        compiler_params=_cparams(("arbitrary", "arbitrary")),
        name="peer_experts_ln",
    )(xs, modv, modv, modv, ln_g, ln_b, u, v, n1, rk2, p1, p2)


def _permute_qk_columns(w):
    d = w.shape[0]
    a_end, b_start, b_end = COL_AV * LANES, COL_BQ * LANES, COL_BV * LANES
    wa = w[:, :a_end].reshape(d, COL_AV, 2, A_QK_DIM // 2, 2)
    wa = wa.transpose(0, 1, 4, 2, 3).reshape(d, a_end)
    wb = w[:, b_start:b_end].reshape(d, COL_BV - COL_BQ, HEAD_DIM // 2, 2)
    wb = wb.transpose(0, 1, 3, 2).reshape(d, b_end - b_start)
    return jnp.concatenate([wa, w[:, a_end:b_start], wb, w[:, b_end:]], axis=1)


def _rope_tables(s_lat, tp):
    t = jnp.arange(s_lat)
    row = (t // GRID_W).astype(F32)
    col = (t % GRID_W).astype(F32)
    lane = np.arange(LANES)
    sign = jnp.asarray(np.where(lane < LANES // 2, -1.0, 1.0), F32)

    def tabs(dim, lane_to_pair):
        d_axis = dim // 2
        inv = ROPE_BASE ** (-jnp.arange(0, d_axis, 2, dtype=F32) / d_axis)
        ang = jnp.concatenate([row[:, None] * inv, col[:, None] * inv], axis=-1)
        ang = ang[:, lane_to_pair]
        pad = ((0, tp - s_lat), (0, 0))
        return (jnp.pad(jnp.cos(ang), pad, constant_values=1.0),
                jnp.pad(jnp.sin(ang) * sign, pad, constant_values=0.0))

    ca, sa = tabs(A_QK_DIM, lane % 32)
    cb, sb = tabs(HEAD_DIM, lane % 64)
    return jnp.stack([ca, cb]), jnp.stack([sa, sb])


def kernel(x, c, ctx, c_ctx, w_ada, b_ada, w_in, w_out, beta_out, ln1_g, ln1_b, ln2_g, ln2_b, diff_lq1, diff_lk1, diff_lq2, diff_lk2, diff_subln, sink, na_rpb, peer_wq, peer_keys, peer_u, peer_v):
    depth = w_ada.shape[0]
    _, s_lat, d = x.shape
    n_ctx = ctx.shape[1]
    assert x.shape[0] == 1 and s_lat % TOKEN_TILE == 0 and n_ctx == 256 and s_lat % n_ctx == 0
    n_lat_tiles = s_lat // TOKEN_TILE
    tp = (n_lat_tiles + 1) * TOKEN_TILE
    alpha = (2 * depth) ** 0.25

    xs = jnp.concatenate([x[0], ctx[0], jnp.zeros((tp - s_lat - n_ctx, d), x.dtype)], axis=0)
    cc = jnp.concatenate([c[0:1], c_ctx[None, :], jnp.zeros((6, d), c.dtype)], axis=0)
    mod = _ada_modulation(cc, w_ada, b_ada)
    modv = mod[:, :2, :].reshape(depth * 2 * 6, 1, d)
    rope_tabs = _rope_tables(s_lat, tp)
    na_tabs = _na_bias_tables(na_rpb.reshape((depth * C_HEADS,) + na_rpb.shape[2:]), s_lat)

    for l in range(depth):
        lam_init = 0.8 - 0.6 * math.exp(-0.3 * l)
        last = l == depth - 1
        w_in_l = _permute_qk_columns(_to_bf16(w_in, l))
        proj, vt = _projection(xs, modv, l, 1, 0, w_in_l, n_lat_tiles, rope_tabs)
        beta = beta_out[l][None, :]
        ma = _attention_a(proj, vt, diff_lq1[l][None], diff_lk1[l][None], diff_lq2[l][None], diff_lk2[l][None],
                          diff_subln[l][None], beta, s_lat, n_ctx, lam_init)
        mb = _attention_b(proj, sink[l], beta, s_lat, n_ctx)
        mc = _attention_c(proj, na_tabs, l * C_HEADS, beta, s_lat, n_ctx)
        xs = _out_projection(ma, mb, mc, _to_bf16(w_out, l), xs, modv, l, n_lat_tiles,
                             ln1_g[l][None], ln1_b[l][None], alpha)
        qp = _projection(xs, modv, l, 4, 3, _to_bf16(peer_wq, l), n_lat_tiles)
        route = _peer_route(qp, peer_keys[l].astype(BF16))
        xs = _peer_experts(xs, modv, l, n_lat_tiles, n_lat_tiles if last else n_lat_tiles + 1,
                           ln2_g[l][None], ln2_b[l][None], _to_bf16(peer_u, l), _to_bf16(peer_v, l),
                           route, alpha)
    return xs[None]
```

```python
import functools
import math

import numpy as np
import jax
import jax.numpy as jnp
from jax import lax
from jax.experimental import pallas as pl
from jax.experimental.pallas import tpu as pltpu

F32 = jnp.float32
BF16 = jnp.bfloat16

GRID_W = 64
HEAD_DIM = 128
A_HEADS = 4
B_HEADS = 8
B_KV_HEADS = 2
B_GROUP = 4
C_HEADS = 4
A_QK_DIM = 64
WINDOW = 128
NA_KH = 8
NA_KW = 16
NA_HALO_ROWS = NA_KH // 2
PEER_HEADS = 8
PEER_KEYS = 128
PEER_TOPK = 16
ROPE_BASE = 10000.0
LN_EPS = 1e-5

LANES = 128
BF16_SUBLANES = 16
TOKEN_TILE = 512
KEY_CHUNK = 256
A_ONES_ROWS = 16
A_Q_SCALE = A_QK_DIM ** -0.5 * math.log2(math.e)
VMEM_LIMIT = 56 * 1024 * 1024
CAST_BLOCK_BYTES = 4 * 1024 * 1024

COL_AQ, COL_AK, COL_AV = 0, 4, 8
COL_BQ, COL_BK, COL_BV = 12, 20, 22
COL_CQ, COL_CK, COL_CV = 24, 28, 32
N_COLBLOCKS = 36


def _cparams(sem):
    return pltpu.CompilerParams(dimension_semantics=sem, vmem_limit_bytes=VMEM_LIMIT)


def _dot_nt(a, b):
    return lax.dot_general(a, b, (((1,), (1,)), ((), ())), preferred_element_type=F32)


def _dot_tn(a, b):
    return lax.dot_general(a, b, (((0,), (0,)), ((), ())), preferred_element_type=F32)


def _layer_norm(y, g, b):
    mu = jnp.mean(y, axis=-1, keepdims=True)
    d = y - mu
    var = jnp.mean(d * d, axis=-1, keepdims=True)
    return d * lax.rsqrt(var + LN_EPS) * g + b


def _cast_kernel(x_ref, o_ref):
    o_ref[...] = x_ref[...].astype(o_ref.dtype)


def _to_bf16(w, layer):
    _, r, c = w.shape
    tr = CAST_BLOCK_BYTES // (4 * c) // 8 * 8
    while r % tr:
        tr -= 8
    return pl.pallas_call(
        _cast_kernel,
        grid=(r // tr,),
        in_specs=[pl.BlockSpec((None, tr, c), lambda i: (layer, i, 0))],
        out_specs=pl.BlockSpec((tr, c), lambda i: (i, 0)),
        out_shape=jax.ShapeDtypeStruct((r, c), BF16),
        compiler_params=_cparams(("arbitrary",)),
        name="weight_to_bf16",
    )(w)


def _ada_kernel(c_ref, w_ref, b_ref, o_ref):
    c = c_ref[...]
    a = c * jax.nn.sigmoid(c)
    o_ref[0] = jnp.dot(a.astype(BF16), w_ref[0].astype(BF16), preferred_element_type=F32) + b_ref[0]


def _ada_modulation(cc, w_ada, b_ada):
    depth, d, n = w_ada.shape
    tn = 1024
    return pl.pallas_call(
        _ada_kernel,
        grid=(depth, n // tn),
        in_specs=[
            pl.BlockSpec((8, d), lambda l, j: (0, 0)),
            pl.BlockSpec((1, d, tn), lambda l, j: (l, 0, j)),
            pl.BlockSpec((1, 1, tn), lambda l, j: (l, 0, j)),
        ],
        out_specs=pl.BlockSpec((1, 8, tn), lambda l, j: (l, 0, j)),
        out_shape=jax.ShapeDtypeStruct((depth, 8, n), F32),
        compiler_params=_cparams(("arbitrary", "arbitrary")),
        name="ada_modulation",
    )(cc, w_ada, b_ada.reshape(depth, 1, n))


def _rope_kind(colblock):
    if colblock < COL_AV:
        return 0
    if COL_BQ <= colblock < COL_BV:
        return 1
    return None


def _proj_kernel(x_ref, sc_ref, sh_ref, w_ref, *rest, tn, rope):
    if rope:
        cos_ref, sin_ref, o_ref, vt_ref = rest
    else:
        (o_ref,) = rest
    xm = (x_ref[...] * (1.0 + sc_ref[0]) + sh_ref[0]).astype(BF16)
    n = w_ref.shape[1]
    for jn in range(n // tn):
        y = jnp.dot(xm, w_ref[:, jn * tn:(jn + 1) * tn], preferred_element_type=F32)
        for jb in range(tn // LANES):
            colblock = jn * (tn // LANES) + jb
            yb = y[:, jb * LANES:(jb + 1) * LANES]
            kind = _rope_kind(colblock) if rope else None
            if kind is not None:
                yb = yb * cos_ref[kind] + pltpu.roll(yb, LANES // 2, 1) * sin_ref[kind]
            if rope and colblock < COL_AK:
                yb = yb * A_Q_SCALE
            o_ref[:, colblock * LANES:(colblock + 1) * LANES] = yb.astype(o_ref.dtype)
            if rope and COL_AV <= colblock < COL_BQ:
                hd = colblock - COL_AV
                for ck in range(yb.shape[0] // KEY_CHUNK):
                    vt_ref[ck, hd * HEAD_DIM:(hd + 1) * HEAD_DIM, :] = (
                        yb[ck * KEY_CHUNK:(ck + 1) * KEY_CHUNK, :].T.astype(vt_ref.dtype))


def _mod_index(layer_slot, chunk, n_lat_tiles):
    def index(i):
        who = jnp.where(i >= n_lat_tiles, 1, 0)
        return ((layer_slot * 2 + who) * 6 + chunk, 0, 0)
    return index


def _projection(xs, modv, layer, chunk_scale, chunk_shift, w, n_lat_tiles, rope_tabs=None):
    tp, d = xs.shape
    n = w.shape[1]
    tm = TOKEN_TILE
    rope = rope_tabs is not None
    in_specs = [
        pl.BlockSpec((tm, d), lambda i: (i, 0)),
        pl.BlockSpec((1, 1, d), _mod_index(layer, chunk_scale, n_lat_tiles)),
        pl.BlockSpec((1, 1, d), _mod_index(layer, chunk_shift, n_lat_tiles)),
        pl.BlockSpec((d, n), lambda i: (0, 0), pipeline_mode=pl.Buffered(1)),
    ]
    args = [xs, modv, modv, w]
    if rope:
        in_specs += [pl.BlockSpec((2, tm, LANES), lambda i: (0, i, 0))] * 2
        args += list(rope_tabs)
    out_specs = pl.BlockSpec((tm, n), lambda i: (i, 0))
    out_shape = jax.ShapeDtypeStruct((tp, n), BF16)
    if rope:
        per = tm // KEY_CHUNK
        out_specs = [out_specs, pl.BlockSpec((per, A_HEADS * HEAD_DIM, KEY_CHUNK), lambda i: (i, 0, 0))]
        out_shape = [out_shape, jax.ShapeDtypeStruct((tp // KEY_CHUNK, A_HEADS * HEAD_DIM, KEY_CHUNK), BF16)]
    return pl.pallas_call(
        functools.partial(_proj_kernel, tn=512, rope=rope),
        grid=(tp // tm,),
        in_specs=in_specs,
        out_specs=out_specs,
        out_shape=out_shape,
        compiler_params=_cparams(("arbitrary",)),
        name="mod_projection_rope" if rope else "mod_projection",
    )(*args)


def _attn_a_kernel(lq1_ref, lk1_ref, lq2_ref, lk2_ref, subln_ref, beta_ref, q_ref, k_ref, vt_ref, o_ref,
                   qs_sc, s0_sc, s1_sc, p0_sc, p1_sc, m_sc, acc_sc, *, s_lat, tq, n_sub, n_lat_tiles, lam_init):
    i = pl.program_id(1)
    lane = lax.broadcasted_iota(jnp.int32, (1, LANES), 1)
    map1 = ((lane // 32) % 2) == 1
    q = q_ref[...]
    zero = jnp.zeros_like(q)
    qs_sc[0:tq, :] = jnp.where(map1, zero, q)
    qs_sc[tq:2 * tq, :] = jnp.where(map1, q, zero)
    m_sc[...] = jnp.full_like(m_sc, -jnp.inf)
    acc_sc[...] = jnp.zeros_like(acc_sc)
    ones_rows = jnp.where(lax.broadcasted_iota(jnp.int32, (A_ONES_ROWS, KEY_CHUNK), 0) == 0, 1.0, 0.0).astype(BF16)

    def keys(c):
        start = c * KEY_CHUNK
        if not isinstance(c, int):
            start = pl.multiple_of(start, KEY_CHUNK)
        return k_ref[pl.ds(start, KEY_CHUNK), :]

    def scores(group, s_sc):
        for j in range(n_sub):
            s_sc[j] = _dot_nt(keys(group * n_sub + j), qs_sc[...])

    def update(chunks, s):
        m_prev = m_sc[...]
        m_new = m_prev
        for sj in s:
            m_new = jnp.maximum(m_new, jnp.max(sj, axis=0, keepdims=True))
        acc = jnp.exp2(m_prev - m_new) * acc_sc[...]
        for c, sj in zip(chunks, s):
            p = jnp.exp2(sj - m_new).astype(BF16)
            v1 = jnp.concatenate([vt_ref[c], ones_rows], axis=0)
            acc = acc + jnp.dot(v1, p, preferred_element_type=F32)
        acc_sc[...] = acc
        m_sc[...] = m_new

    ctx_chunk = s_lat // KEY_CHUNK

    def context():
        update([ctx_chunk], [_dot_nt(keys(ctx_chunk), qs_sc[...])])

    @pl.when(i >= n_lat_tiles)
    def _():
        context()

    def values(group, p_sc):
        out = None
        for j in range(n_sub):
            v1 = jnp.concatenate([vt_ref[group * n_sub + j], ones_rows], axis=0)
            part = jnp.dot(v1, p_sc[j], preferred_element_type=F32)
            out = part if out is None else out + part
        return out

    def stage(g_scores, s_next_sc, g_values, p_prev_sc, s_cur_sc, p_cur_sc):
        acc_sc[...] += values(g_values, p_prev_sc)
        scores(g_scores, s_next_sc)
        m_prev = m_sc[...]
        m_new = m_prev
        for j in range(n_sub):
            m_new = jnp.maximum(m_new, jnp.max(s_cur_sc[j], axis=0, keepdims=True))
        for j in range(n_sub):
            p_cur_sc[j] = jnp.exp2(s_cur_sc[j] - m_new).astype(BF16)
        acc_sc[...] *= jnp.exp2(m_prev - m_new)
        m_sc[...] = m_new

    @pl.when(i < n_lat_tiles)
    def _():
        n_groups = s_lat // (KEY_CHUNK * n_sub)
        scores(0, s0_sc)
        context()
        p1_sc[...] = jnp.zeros_like(p1_sc)

        unroll = math.gcd(n_groups, 8)
        assert unroll % 2 == 0

        def body(gu, carry):
            for u in range(unroll):
                g = unroll * gu + u
                bufs = (s1_sc, p1_sc, s0_sc, p0_sc) if u % 2 == 0 else (s0_sc, p0_sc, s1_sc, p1_sc)
                s_next, p_prev, s_cur, p_cur = bufs
                stage(jnp.minimum(g + 1, n_groups - 1), s_next, jnp.maximum(g - 1, 0), p_prev, s_cur, p_cur)
            return carry
        lax.fori_loop(0, n_groups // unroll, body, 0)
        acc_sc[...] += values(n_groups - 1, p1_sc)

    lam = (jnp.exp(jnp.sum(lq1_ref[...] * lk1_ref[...], axis=-1, keepdims=True))
           - jnp.exp(jnp.sum(lq2_ref[...] * lk2_ref[...], axis=-1, keepdims=True)) + lam_init)
    l = acc_sc[HEAD_DIM:HEAD_DIM + 1, :]
    o_t = (acc_sc[0:HEAD_DIM, 0:tq] / l[:, 0:tq] - lam * (acc_sc[0:HEAD_DIM, tq:2 * tq] / l[:, tq:2 * tq]))
    o = o_t.T
    y = o * lax.rsqrt(jnp.mean(o * o, axis=-1, keepdims=True) + LN_EPS) * subln_ref[...]
    o_ref[...] = (y * (1.0 - lam_init) * beta_ref[...]).astype(o_ref.dtype)


def _attention_a(proj, vt, lq1, lk1, lq2, lk2, subln, beta, s_lat, n_ctx, lam_init):
    tp = proj.shape[0]
    assert n_ctx == KEY_CHUNK
    tq, n_sub = 256, 2
    n_lat_tiles = s_lat // tq
    vec = lambda: pl.BlockSpec((1, A_QK_DIM), lambda h, i: (0, 0))
    return pl.pallas_call(
        functools.partial(_attn_a_kernel, s_lat=s_lat, tq=tq, n_sub=n_sub, n_lat_tiles=n_lat_tiles,
                          lam_init=lam_init),
        grid=(A_HEADS, tp // tq),
        in_specs=[
            vec(), vec(), vec(), vec(),
            pl.BlockSpec((1, HEAD_DIM), lambda h, i: (0, 0)),
            pl.BlockSpec((1, HEAD_DIM), lambda h, i: (0, h)),
            pl.BlockSpec((tq, HEAD_DIM), lambda h, i: (i, COL_AQ + h)),
            pl.BlockSpec((tp, HEAD_DIM), lambda h, i: (0, COL_AK + h)),
            pl.BlockSpec((tp // KEY_CHUNK, HEAD_DIM, KEY_CHUNK), lambda h, i: (0, h, 0)),
        ],
        out_specs=pl.BlockSpec((tq, HEAD_DIM), lambda h, i: (i, h)),
        out_shape=jax.ShapeDtypeStruct((tp, A_HEADS * HEAD_DIM), BF16),
        scratch_shapes=[
            pltpu.VMEM((2 * tq, HEAD_DIM), BF16),
            pltpu.VMEM((n_sub, KEY_CHUNK, 2 * tq), F32),
            pltpu.VMEM((n_sub, KEY_CHUNK, 2 * tq), F32),
            pltpu.VMEM((n_sub, KEY_CHUNK, 2 * tq), BF16),
            pltpu.VMEM((n_sub, KEY_CHUNK, 2 * tq), BF16),
            pltpu.VMEM((1, 2 * tq), F32),
            pltpu.VMEM((HEAD_DIM + A_ONES_ROWS, 2 * tq), F32),
        ],
        compiler_params=_cparams(("arbitrary", "arbitrary")),
        name="attn_diff",
    )(lq1, lk1, lq2, lk2, subln, beta, proj, proj, vt)


def _attn_b_kernel(sink_ref, beta_ref, q_ref, kp_ref, kc_ref, kn_ref, kx_ref, vp_ref, vc_ref, vn_ref, vx_ref,
                   o_ref, *, s_lat, tq):
    kv = pl.program_id(0)
    i = pl.program_id(1)
    scale = HEAD_DIM ** -0.5
    q0 = i * tq
    qpos = q0 + lax.broadcasted_iota(jnp.int32, (tq, 1), 0)

    def valid(kstart, n):
        kpos = kstart + lax.broadcasted_iota(jnp.int32, (1, n), 1)
        return (jnp.abs(kpos - qpos) <= WINDOW) & (kpos >= 0) & (kpos < s_lat) & (qpos < s_lat)

    ok_p = valid(q0 - WINDOW, WINDOW)
    ok_c = valid(q0, tq)
    ok_n = valid(q0 + tq, WINDOW)
    neg = -jnp.inf
    for g in range(B_GROUP):
        qg = q_ref[:, g * HEAD_DIM:(g + 1) * HEAD_DIM]
        s_x = _dot_nt(qg, kx_ref[...]) * scale
        s_p = jnp.where(ok_p, _dot_nt(qg, kp_ref[...]) * scale, neg)
        s_c = jnp.where(ok_c, _dot_nt(qg, kc_ref[...]) * scale, neg)
        s_n = jnp.where(ok_n, _dot_nt(qg, kn_ref[...]) * scale, neg)
        snk = sink_ref[kv * B_GROUP + g]
        mx = jnp.maximum(jnp.maximum(jnp.max(s_x, -1, keepdims=True), jnp.max(s_p, -1, keepdims=True)),
                         jnp.maximum(jnp.max(s_c, -1, keepdims=True), jnp.max(s_n, -1, keepdims=True)))
        mx = jnp.maximum(mx, snk)
        e_x = jnp.exp(s_x - mx)
        e_p = jnp.exp(s_p - mx)
        e_c = jnp.exp(s_c - mx)
        e_n = jnp.exp(s_n - mx)
        den = (jnp.exp(snk - mx) + jnp.sum(e_x, -1, keepdims=True) + jnp.sum(e_p, -1, keepdims=True)
               + jnp.sum(e_c, -1, keepdims=True) + jnp.sum(e_n, -1, keepdims=True))
        o = (jnp.dot(e_x.astype(BF16), vx_ref[...], preferred_element_type=F32)
             + jnp.dot(e_p.astype(BF16), vp_ref[...], preferred_element_type=F32)
             + jnp.dot(e_c.astype(BF16), vc_ref[...], preferred_element_type=F32)
             + jnp.dot(e_n.astype(BF16), vn_ref[...], preferred_element_type=F32))
        o_ref[:, g * HEAD_DIM:(g + 1) * HEAD_DIM] = (
            (o / den) * beta_ref[:, g * HEAD_DIM:(g + 1) * HEAD_DIM]).astype(o_ref.dtype)


def _attention_b(proj, sink, beta, s_lat, n_ctx):
    tp = proj.shape[0]
    tq = 256
    per = tq // WINDOW
    last_halo = tp // WINDOW - 1
    gw = B_GROUP * HEAD_DIM

    def halo_prev(col):
        return pl.BlockSpec((WINDOW, HEAD_DIM), lambda kv, i: (jnp.maximum(i * per - 1, 0), col + kv))

    def halo_next(col):
        return pl.BlockSpec((WINDOW, HEAD_DIM), lambda kv, i: (jnp.minimum((i + 1) * per, last_halo), col + kv))

    def cur(col):
        return pl.BlockSpec((tq, HEAD_DIM), lambda kv, i: (i, col + kv))

    def ctx(col):
        return pl.BlockSpec((n_ctx, HEAD_DIM), lambda kv, i: (s_lat // n_ctx, col + kv))

    return pl.pallas_call(
        functools.partial(_attn_b_kernel, s_lat=s_lat, tq=tq),
        grid=(B_KV_HEADS, tp // tq),
        in_specs=[
            pl.BlockSpec(memory_space=pltpu.SMEM),
            pl.BlockSpec((1, gw), lambda kv, i: (0, (A_HEADS * HEAD_DIM) // gw + kv)),
            pl.BlockSpec((tq, gw), lambda kv, i: (i, (COL_BQ * LANES) // gw + kv)),
            halo_prev(COL_BK), cur(COL_BK), halo_next(COL_BK), ctx(COL_BK),
            halo_prev(COL_BV), cur(COL_BV), halo_next(COL_BV), ctx(COL_BV),
        ],
        out_specs=pl.BlockSpec((tq, gw), lambda kv, i: (i, kv)),
        out_shape=jax.ShapeDtypeStruct((tp, B_HEADS * HEAD_DIM), BF16),
        compiler_params=_cparams(("arbitrary", "arbitrary")),
        name="attn_window",
    )(sink, beta, proj, proj, proj, proj, proj, proj, proj, proj, proj)


def _na_bias_tables(rpb, s_lat):
    rows = s_lat // GRID_W
    tr = TOKEN_TILE // GRID_W
    kh = min(NA_KH, rows)

    neg = -1e30
    n_heads = rpb.shape[0]
    per_c = []
    for c in range(GRID_W):
        c0 = min(max(c - NA_KW // 2, 0), GRID_W - NA_KW)
        lo = c0 - c + NA_KW - 1
        per_c.append(jnp.pad(rpb[:, :, lo:lo + NA_KW], ((0, 0), (0, 0), (c0, GRID_W - NA_KW - c0)),
                             constant_values=neg))
    colb = jnp.stack(per_c, axis=2).astype(F32)
    off = jnp.full((n_heads, GRID_W, GRID_W), neg, F32)

    hr = NA_HALO_ROWS

    def table(q_row0, n_rows):
        per_r = []
        for r in range(q_row0, q_row0 + tr):
            r0 = min(max(r - kh // 2, 0), n_rows - kh)
            assert q_row0 - hr <= r0 and r0 + kh <= q_row0 + tr + hr
            blocks = []
            for kr in range(q_row0 - hr, q_row0 + tr + hr):
                inside = r0 <= kr < r0 + kh and 0 <= kr < n_rows
                blocks.append(colb[:, kr - r + NA_KH - 1] if inside else off)
            per_r.append(jnp.concatenate(blocks, axis=-1))
        return jnp.stack(per_r, axis=1).reshape(n_heads, TOKEN_TILE, TOKEN_TILE + 2 * hr * GRID_W)

    far = 4 * tr
    out = [table(0, rows), table(far, 2 * far + tr), table(rows - tr, rows)]
    out.append(jnp.full_like(out[0], neg))
    return jnp.stack(out, axis=0)


def _attn_c_kernel(beta_ref, bias_ref, q_ref, kp_ref, kc_ref, kn_ref, kx_ref, vp_ref, vc_ref, vn_ref, vx_ref,
                   o_ref, *, tq):
    scale = HEAD_DIM ** -0.5
    q = q_ref[...]
    hq = kp_ref.shape[0]
    s_x = _dot_nt(q, kx_ref[...]) * scale
    s_p = _dot_nt(q, kp_ref[...]) * scale + bias_ref[0, 0, :, 0:hq]
    s_c = _dot_nt(q, kc_ref[...]) * scale + bias_ref[0, 0, :, hq:hq + tq]
    s_n = _dot_nt(q, kn_ref[...]) * scale + bias_ref[0, 0, :, hq + tq:2 * hq + tq]
    mx = jnp.maximum(jnp.maximum(jnp.max(s_x, -1, keepdims=True), jnp.max(s_p, -1, keepdims=True)),
                     jnp.maximum(jnp.max(s_c, -1, keepdims=True), jnp.max(s_n, -1, keepdims=True)))
    e_x = jnp.exp(s_x - mx)
    e_p = jnp.exp(s_p - mx)
    e_c = jnp.exp(s_c - mx)
    e_n = jnp.exp(s_n - mx)
    den = (jnp.sum(e_x, -1, keepdims=True) + jnp.sum(e_p, -1, keepdims=True)
           + jnp.sum(e_c, -1, keepdims=True) + jnp.sum(e_n, -1, keepdims=True))
    o = (jnp.dot(e_x.astype(BF16), vx_ref[...], preferred_element_type=F32)
         + jnp.dot(e_p.astype(BF16), vp_ref[...], preferred_element_type=F32)
         + jnp.dot(e_c.astype(BF16), vc_ref[...], preferred_element_type=F32)
         + jnp.dot(e_n.astype(BF16), vn_ref[...], preferred_element_type=F32))
    o_ref[...] = ((o / den) * beta_ref[...]).astype(o_ref.dtype)


def _attention_c(proj, bias_tabs, bias_head0, beta, s_lat, n_ctx):
    tp = proj.shape[0]
    tq = TOKEN_TILE
    hq = NA_HALO_ROWS * GRID_W
    per = tq // hq
    n_lat_tiles = s_lat // tq
    n_tiles = tp // tq
    beta_col0 = (A_HEADS + B_HEADS)

    def kind(i):
        return jnp.where(i >= n_lat_tiles, 3, jnp.where(i == 0, 0, jnp.where(i == n_lat_tiles - 1, 2, 1)))

    def prev(col):
        return pl.BlockSpec((hq, HEAD_DIM), lambda h, i: (jnp.maximum(i * per - 1, 0), col + h))

    def cur(col):
        return pl.BlockSpec((tq, HEAD_DIM), lambda h, i: (i, col + h))

    def nxt(col):
        return pl.BlockSpec((hq, HEAD_DIM), lambda h, i: (jnp.minimum((i + 1) * per, tp // hq - 1), col + h))

    def ctx(col):
        return pl.BlockSpec((n_ctx, HEAD_DIM), lambda h, i: (s_lat // n_ctx, col + h))

    return pl.pallas_call(
        functools.partial(_attn_c_kernel, tq=tq),
        grid=(C_HEADS, n_tiles),
        in_specs=[
            pl.BlockSpec((1, HEAD_DIM), lambda h, i: (0, beta_col0 + h)),
            pl.BlockSpec((1, 1, tq, tq + 2 * hq), lambda h, i: (kind(i), bias_head0 + h, 0, 0)),
            cur(COL_CQ),
            prev(COL_CK), cur(COL_CK), nxt(COL_CK), ctx(COL_CK),
            prev(COL_CV), cur(COL_CV), nxt(COL_CV), ctx(COL_CV),
        ],
        out_specs=pl.BlockSpec((tq, HEAD_DIM), lambda h, i: (i, h)),
        out_shape=jax.ShapeDtypeStruct((tp, C_HEADS * HEAD_DIM), BF16),
        compiler_params=_cparams(("arbitrary", "arbitrary")),
        name="attn_neighbourhood",
    )(beta, bias_tabs, proj, proj, proj, proj, proj, proj, proj, proj, proj)


def _outproj_kernel(ma_ref, mb_ref, mc_ref, w_ref, x_ref, gate_ref, g_ref, b_ref, o_ref, *, alpha):
    wa = A_HEADS * HEAD_DIM
    wb = wa + B_HEADS * HEAD_DIM
    y = (jnp.dot(ma_ref[...], w_ref[0:wa, :], preferred_element_type=F32)
         + jnp.dot(mb_ref[...], w_ref[wa:wb, :], preferred_element_type=F32)
         + jnp.dot(mc_ref[...], w_ref[wb:, :], preferred_element_type=F32))
    z = alpha * x_ref[...] + gate_ref[0] * y
    o_ref[...] = _layer_norm(z, g_ref[...], b_ref[...])


def _out_projection(ma, mb, mc, w_out, xs, modv, layer, n_lat_tiles, ln_g, ln_b, alpha):
    tp, d = xs.shape
    tm = TOKEN_TILE // 2
    n_lat = n_lat_tiles * (TOKEN_TILE // tm)
    row = lambda i: (i, 0)
    fixed = lambda i: (0, 0)
    return pl.pallas_call(
        functools.partial(_outproj_kernel, alpha=alpha),
        grid=(tp // tm,),
        in_specs=[
            pl.BlockSpec((tm, ma.shape[1]), row),
            pl.BlockSpec((tm, mb.shape[1]), row),
            pl.BlockSpec((tm, mc.shape[1]), row),
            pl.BlockSpec(w_out.shape, fixed, pipeline_mode=pl.Buffered(1)),
            pl.BlockSpec((tm, d), row),
            pl.BlockSpec((1, 1, d), _mod_index(layer, 2, n_lat)),
            pl.BlockSpec((1, d), fixed),
            pl.BlockSpec((1, d), fixed),
        ],
        out_specs=pl.BlockSpec((tm, d), row),
        out_shape=jax.ShapeDtypeStruct((tp, d), F32),
        compiler_params=_cparams(("arbitrary",)),
        name="out_projection_ln",
    )(ma, mb, mc, w_out, xs, modv, ln_g, ln_b)


def _take_top(vals, rounds, want_round):
    tops = []
    n, t = vals.shape
    rnd = jnp.full((n, t), float(n), F32) if want_round else None
    for r in range(rounds):
        m = jnp.max(vals, axis=0, keepdims=True)
        eq = vals == m
        if want_round:
            rnd = jnp.minimum(rnd, jnp.where(eq, float(r), float(n)))
        tops.append(m)
        vals = jnp.where(eq, -jnp.inf, vals)
    removed = jnp.sum(jnp.where(vals == -jnp.inf, 1.0, 0.0), axis=0, keepdims=True)
    return jnp.concatenate(tops, axis=0), removed, rnd


def _multiplicities(s, v):
    return jnp.concatenate([jnp.sum(jnp.where(s == v[r:r + 1], 1.0, 0.0), axis=0, keepdims=True)
                            for r in range(v.shape[0])], axis=0)


def _pair_table(a, b, op):
    half = PEER_TOPK // 2
    pieces = [op(a[0:1], b)] + [op(a[r:r + 1], b[0:half]) for r in range(1, half)] + [op(a[half:], b[0:1])]
    return jnp.concatenate(pieces, axis=0)


def _route_kernel(q_ref, keys_ref, n1_ref, rk2_ref, p1_ref, p2_ref, s_sc, v_sc, c_sc):
    k = PEER_TOPK
    tt = q_ref.shape[0]
    chunks = [slice(lc * LANES, (lc + 1) * LANES) for lc in range(tt // LANES)]
    for h in range(PEER_HEADS):
        s_sc[0] = _dot_nt(keys_ref[h, 0], q_ref[:, (2 * h) * LANES:(2 * h + 1) * LANES])
        s_sc[1] = _dot_nt(keys_ref[h, 1], q_ref[:, (2 * h + 1) * LANES:(2 * h + 2) * LANES])
        most_removed = jnp.zeros((1, LANES), F32)
        for cols in chunks:
            v1, removed1, _ = _take_top(s_sc[0, :, cols], k, False)
            v2, removed2, round2 = _take_top(s_sc[1, :, cols], k, True)
            v_sc[0, :, cols] = v1
            v_sc[1, :, cols] = v2
            rk2_ref[h, :, cols] = round2.astype(BF16)
            most_removed = jnp.maximum(most_removed, jnp.maximum(removed1, removed2))
        c_sc[...] = jnp.ones_like(c_sc)

        @pl.when(jnp.max(most_removed) > k)
        def _():
            for cols in chunks:
                s2 = s_sc[1, :, cols]
                v2 = v_sc[1, :, cols]
                c2 = _multiplicities(s2, v2)
                above = jnp.zeros_like(s2)
                for r in range(k):
                    above = above + jnp.where(v2[r:r + 1] > s2, c2[r:r + 1], 0.0)
                c_sc[0, :, cols] = _multiplicities(s_sc[0, :, cols], v_sc[0, :, cols])
                c_sc[1, :, cols] = c2
                rk2_ref[h, :, cols] = above.astype(BF16)

        for cols in chunks:
            s1 = s_sc[0, :, cols]
            s2 = s_sc[1, :, cols]
            v1 = v_sc[0, :, cols]
            v2 = v_sc[1, :, cols]
            c1 = c_sc[0, :, cols]
            c2 = c_sc[1, :, cols]
            cand = _pair_table(v1, v2, jnp.add)
            mult = _pair_table(c1, c2, jnp.multiply)
            thr = jnp.full((1, LANES), jnp.inf, F32)
            seen = jnp.zeros((1, LANES), F32)
            rem = cand
            for _ in range(k):
                m = jnp.max(rem, axis=0, keepdims=True)
                eq = rem == m
                thr = jnp.where(seen < k, m, thr)
                seen = seen + jnp.sum(jnp.where(eq, mult, 0.0), axis=0, keepdims=True)
                rem = jnp.where(eq, -jnp.inf, rem)
            pair = _pair_table(jnp.exp(v1 - v1[0:1]), jnp.exp(v2 - v2[0:1]), jnp.multiply)
            z = jnp.sum(jnp.where(cand >= thr, mult * pair, 0.0), axis=0, keepdims=True)
            n1 = jnp.zeros_like(s1)
            for r in range(k):
                n1 = n1 + jnp.where(s1 + v2[r:r + 1] >= thr, c2[r:r + 1], 0.0)
            n1_ref[h, :, cols] = n1
            p1_ref[h, :, cols] = jnp.exp(s1 - v1[0:1])
            p2_ref[h, :, cols] = (jnp.exp(s2 - v2[0:1]) / z).astype(BF16)


def _peer_route(qp, keys):
    tp = qp.shape[0]
    tt = TOKEN_TILE
    big = pl.BlockSpec((PEER_HEADS, PEER_KEYS, tt), lambda j: (0, 0, j))
    wide = jax.ShapeDtypeStruct((PEER_HEADS, PEER_KEYS, tp), F32)
    narrow = jax.ShapeDtypeStruct((PEER_HEADS, PEER_KEYS, tp), BF16)
    return pl.pallas_call(
        _route_kernel,
        grid=(tp // tt,),
        in_specs=[
            pl.BlockSpec((tt, qp.shape[1]), lambda j: (j, 0)),
            pl.BlockSpec(keys.shape, lambda j: (0, 0, 0, 0)),
        ],
        out_specs=[big, big, big, big],
        out_shape=[wide, narrow, wide, narrow],
        scratch_shapes=[
            pltpu.VMEM((2, PEER_KEYS, tt), F32),
            pltpu.VMEM((2, PEER_TOPK, tt), F32),
            pltpu.VMEM((2, PEER_TOPK, tt), F32),
        ],
        compiler_params=_cparams(("arbitrary",)),
        name="peer_route",
    )(qp, keys)


def _gelu(a):
    return 0.5 * a * (1.0 + lax.erf(a * (2.0 ** -0.5)))


def _peer_expert_kernel(x_ref, sc_ref, sh_ref, gate_ref, g_ref, b_ref, u_ref, v_ref,
                        n1_ref, rk2_ref, p1_ref, p2_ref, o_ref, ux_sc, w0_sc, w1_sc, at0_sc, at1_sc, acc_sc,
                        *, alpha, te, tt, n_e):
    i = pl.program_id(1)
    pack = BF16_SUBLANES
    d = acc_sc.shape[1]
    n_blk = te // PEER_KEYS

    @pl.when(i == 0)
    def _():
        ux_sc[...] = (x_ref[...] * (1.0 + sc_ref[0]) + sh_ref[0]).astype(BF16)
        acc_sc[...] = jnp.zeros_like(acc_sc)
        w1_sc[...] = jnp.zeros_like(w1_sc)

    zero = jnp.zeros((pack, LANES), BF16)

    at_bufs = (at0_sc, at1_sc)
    eg = at0_sc.shape[0] // PEER_KEYS
    tg = at0_sc.shape[1] // LANES
    n_eg, n_tg = n_blk // eg, tt // LANES // tg
    n_pieces = n_eg * n_tg
    dn = d // n_pieces

    def hidden(r):
        ge, gt = divmod(r, n_tg)
        at_bufs[r % 2][...] = _dot_nt(u_ref[ge * eg * PEER_KEYS:(ge + 1) * eg * PEER_KEYS, :],
                                      ux_sc[gt * tg * LANES:(gt + 1) * tg * LANES, :])

    def cell(cur_sc, r, ca, cl):
        ge, gt = divmod(r, n_tg)
        a, lc = ge * eg + ca, gt * tg + cl
        cols = slice(lc * LANES, (lc + 1) * LANES)
        a_t = at_bufs[r % 2].at[ca * PEER_KEYS:(ca + 1) * PEER_KEYS, cl * LANES:(cl + 1) * LANES]
        n1 = [jnp.broadcast_to(n1_ref[h, a:a + 1, cols], (pack, LANES)).astype(BF16) for h in range(PEER_HEADS)]
        p1 = [jnp.broadcast_to(p1_ref[h, a:a + 1, cols], (pack, LANES)).astype(BF16) for h in range(PEER_HEADS)]
        pieces = []
        for rg in range(PEER_KEYS // pack):
            rows = slice(rg * pack, (rg + 1) * pack)
            g = zero
            for h in range(PEER_HEADS):
                sel_p1 = jnp.minimum(jnp.maximum(n1[h] - rk2_ref[h, rows, cols], zero), p1[h])
                g = g + sel_p1 * p2_ref[h, rows, cols]
            pieces.append(g * _gelu(a_t[rows, :]).astype(BF16))
        blk = jnp.concatenate(pieces, axis=0)
        cur_sc[cols, a * PEER_KEYS:(a + 1) * PEER_KEYS] = blk.T

    def step(cur_sc, prev_sc):
        hidden(0)
        for r in range(n_pieces):
            cells = [(ca, cl) for ca in range(eg) for cl in range(tg)]
            half = len(cells) // 2
            if r + 1 < n_pieces:
                hidden(r + 1)
            for ca, cl in cells[:half]:
                cell(cur_sc, r, ca, cl)
            acc_sc[:, r * dn:(r + 1) * dn] += jnp.dot(prev_sc[...], v_ref[:, r * dn:(r + 1) * dn],
                                                      preferred_element_type=F32)
            for ca, cl in cells[half:]:
                cell(cur_sc, r, ca, cl)

    @pl.when((i < n_e) & (i % 2 == 0))
    def _():
        step(w0_sc, w1_sc)

    @pl.when((i < n_e) & (i % 2 == 1))
    def _():
        step(w1_sc, w0_sc)

    @pl.when(i == n_e)
    def _():
        last_sc = w1_sc if n_e % 2 == 0 else w0_sc
        y = acc_sc[...] + jnp.dot(last_sc[...], v_ref[...], preferred_element_type=F32)
        z = alpha * x_ref[...] + gate_ref[0] * y
        o_ref[...] = _layer_norm(z, g_ref[...], b_ref[...])


def _peer_experts(xs, modv, layer, n_lat_tiles, n_out_tiles, ln_g, ln_b, u, v, route, alpha):
    tp, d = xs.shape
    tt = TOKEN_TILE
    te = 8 * PEER_KEYS
    n_exp = u.shape[0]
    n1, rk2, p1, p2 = route
    once = pl.Buffered(1)
    tok = lambda j, i: (j, 0)
    fixed = lambda j, i: (0, 0)
    mod = lambda chunk: (lambda j, i: _mod_index(layer, chunk, n_lat_tiles)(j))
    rt = lambda: pl.BlockSpec((PEER_HEADS, PEER_KEYS, tt), lambda j, i: (0, 0, j), pipeline_mode=once)
    n_e = n_exp // te
    build = lambda i: jnp.minimum(i, n_e - 1)
    drain = lambda i: jnp.maximum(i - 1, 0)
    rt1 = lambda: pl.BlockSpec((PEER_HEADS, te // PEER_KEYS, tt), lambda j, i: (0, build(i), j))
    return pl.pallas_call(
        functools.partial(_peer_expert_kernel, alpha=alpha, te=te, tt=tt, n_e=n_e),
        grid=(n_out_tiles, n_e + 1),
        in_specs=[
            pl.BlockSpec((tt, d), tok, pipeline_mode=once),
            pl.BlockSpec((1, 1, d), mod(4)),
            pl.BlockSpec((1, 1, d), mod(3)),
            pl.BlockSpec((1, 1, d), mod(5)),
            pl.BlockSpec((1, d), fixed),
            pl.BlockSpec((1, d), fixed),
            pl.BlockSpec((te, d), lambda j, i: (build(i), 0)),
            pl.BlockSpec((te, d), lambda j, i: (drain(i), 0)),
            rt1(), rt(), rt1(), rt(),
        ],
        out_specs=pl.BlockSpec((tt, d), tok),
        out_shape=jax.ShapeDtypeStruct((n_out_tiles * tt, d), F32),
        scratch_shapes=[
            pltpu.VMEM((tt, d), BF16),
            pltpu.VMEM((tt, te), BF16),
            pltpu.VMEM((tt, te), BF16),
            pltpu.VMEM((2 * PEER_KEYS, 2 * LANES), F32),
            pltpu.VMEM((2 * PEER_KEYS, 2 * LANES), F32),
            pltpu.VMEM((tt, d), F32),
        ],
        compiler_params=_cparams(("arbitrary", "arbitrary")),
        name="peer_experts_ln",
    )(xs, modv, modv, modv, ln_g, ln_b, u, v, n1, rk2, p1, p2)


def _permute_qk_columns(w):
    d = w.shape[0]
    a_end, b_start, b_end = COL_AV * LANES, COL_BQ * LANES, COL_BV * LANES
    wa = w[:, :a_end].reshape(d, COL_AV, 2, A_QK_DIM // 2, 2)
    wa = wa.transpose(0, 1, 4, 2, 3).reshape(d, a_end)
    wb = w[:, b_start:b_end].reshape(d, COL_BV - COL_BQ, HEAD_DIM // 2, 2)
    wb = wb.transpose(0, 1, 3, 2).reshape(d, b_end - b_start)
    return jnp.concatenate([wa, w[:, a_end:b_start], wb, w[:, b_end:]], axis=1)


def _rope_tables(s_lat, tp):
    t = jnp.arange(s_lat)
    row = (t // GRID_W).astype(F32)
    col = (t % GRID_W).astype(F32)
    lane = np.arange(LANES)
    sign = jnp.asarray(np.where(lane < LANES // 2, -1.0, 1.0), F32)

    def tabs(dim, lane_to_pair):
        d_axis = dim // 2
        inv = ROPE_BASE ** (-jnp.arange(0, d_axis, 2, dtype=F32) / d_axis)
        ang = jnp.concatenate([row[:, None] * inv, col[:, None] * inv], axis=-1)
        ang = ang[:, lane_to_pair]
        pad = ((0, tp - s_lat), (0, 0))
        return (jnp.pad(jnp.cos(ang), pad, constant_values=1.0),
                jnp.pad(jnp.sin(ang) * sign, pad, constant_values=0.0))

    ca, sa = tabs(A_QK_DIM, lane % 32)
    cb, sb = tabs(HEAD_DIM, lane % 64)
    return jnp.stack([ca, cb]), jnp.stack([sa, sb])


def kernel(x, c, ctx, c_ctx, w_ada, b_ada, w_in, w_out, beta_out, ln1_g, ln1_b, ln2_g, ln2_b, diff_lq1, diff_lk1, diff_lq2, diff_lk2, diff_subln, sink, na_rpb, peer_wq, peer_keys, peer_u, peer_v):
    depth = w_ada.shape[0]
    _, s_lat, d = x.shape
    n_ctx = ctx.shape[1]
    assert x.shape[0] == 1 and s_lat % TOKEN_TILE == 0 and n_ctx == 256 and s_lat % n_ctx == 0
    n_lat_tiles = s_lat // TOKEN_TILE
    tp = (n_lat_tiles + 1) * TOKEN_TILE
    alpha = (2 * depth) ** 0.25

    xs = jnp.concatenate([x[0], ctx[0], jnp.zeros((tp - s_lat - n_ctx, d), x.dtype)], axis=0)
    cc = jnp.concatenate([c[0:1], c_ctx[None, :], jnp.zeros((6, d), c.dtype)], axis=0)
    mod = _ada_modulation(cc, w_ada, b_ada)
    modv = mod[:, :2, :].reshape(depth * 2 * 6, 1, d)
    rope_tabs = _rope_tables(s_lat, tp)
    na_tabs = _na_bias_tables(na_rpb.reshape((depth * C_HEADS,) + na_rpb.shape[2:]), s_lat)

    for l in range(depth):
        lam_init = 0.8 - 0.6 * math.exp(-0.3 * l)
        last = l == depth - 1
        w_in_l = _permute_qk_columns(_to_bf16(w_in, l))
        proj, vt = _projection(xs, modv, l, 1, 0, w_in_l, n_lat_tiles, rope_tabs)
        beta = beta_out[l][None, :]
        ma = _attention_a(proj, vt, diff_lq1[l][None], diff_lk1[l][None], diff_lq2[l][None], diff_lk2[l][None],
                          diff_subln[l][None], beta, s_lat, n_ctx, lam_init)
        mb = _attention_b(proj, sink[l], beta, s_lat, n_ctx)
        mc = _attention_c(proj, na_tabs, l * C_HEADS, beta, s_lat, n_ctx)
        xs = _out_projection(ma, mb, mc, _to_bf16(w_out, l), xs, modv, l, n_lat_tiles,
                             ln1_g[l][None], ln1_b[l][None], alpha)
        qp = _projection(xs, modv, l, 4, 3, _to_bf16(peer_wq, l), n_lat_tiles)
        route = _peer_route(qp, peer_keys[l].astype(BF16))
        xs = _peer_experts(xs, modv, l, n_lat_tiles, n_lat_tiles if last else n_lat_tiles + 1,
                           ln2_g[l][None], ln2_b[l][None], _to_bf16(peer_u, l), _to_bf16(peer_v, l),
                           route, alpha)
    return xs[None]
```

```python
import functools
import math

import numpy as np
import jax
import jax.numpy as jnp
from jax import lax
from jax.experimental import pallas as pl
from jax.experimental.pallas import tpu as pltpu

F32 = jnp.float32
BF16 = jnp.bfloat16

GRID_W = 64
HEAD_DIM = 128
A_HEADS = 4
B_HEADS = 8
B_KV_HEADS = 2
B_GROUP = 4
C_HEADS = 4
A_QK_DIM = 64
WINDOW = 128
NA_KH = 8
NA_KW = 16
NA_HALO_ROWS = NA_KH // 2
PEER_HEADS = 8
PEER_KEYS = 128
PEER_TOPK = 16
ROPE_BASE = 10000.0
LN_EPS = 1e-5

LANES = 128
BF16_SUBLANES = 16
TOKEN_TILE = 512
KEY_CHUNK = 256
A_ONES_ROWS = 16
A_Q_SCALE = A_QK_DIM ** -0.5 * math.log2(math.e)
VMEM_LIMIT = 56 * 1024 * 1024
CAST_BLOCK_BYTES = 4 * 1024 * 1024

COL_AQ, COL_AK, COL_AV = 0, 4, 8
COL_BQ, COL_BK, COL_BV = 12, 20, 22
COL_CQ, COL_CK, COL_CV = 24, 28, 32
N_COLBLOCKS = 36


def _cparams(sem):
    return pltpu.CompilerParams(dimension_semantics=sem, vmem_limit_bytes=VMEM_LIMIT)


def _dot_nt(a, b):
    return lax.dot_general(a, b, (((1,), (1,)), ((), ())), preferred_element_type=F32)


def _dot_tn(a, b):
    return lax.dot_general(a, b, (((0,), (0,)), ((), ())), preferred_element_type=F32)


def _layer_norm(y, g, b):
    mu = jnp.mean(y, axis=-1, keepdims=True)
    d = y - mu
    var = jnp.mean(d * d, axis=-1, keepdims=True)
    return d * lax.rsqrt(var + LN_EPS) * g + b


def _cast_kernel(x_ref, o_ref):
    o_ref[...] = x_ref[...].astype(o_ref.dtype)


def _to_bf16(w, layer):
    _, r, c = w.shape
    tr = CAST_BLOCK_BYTES // (4 * c) // 8 * 8
    while r % tr:
        tr -= 8
    return pl.pallas_call(
        _cast_kernel,
        grid=(r // tr,),
        in_specs=[pl.BlockSpec((None, tr, c), lambda i: (layer, i, 0))],
        out_specs=pl.BlockSpec((tr, c), lambda i: (i, 0)),
        out_shape=jax.ShapeDtypeStruct((r, c), BF16),
        compiler_params=_cparams(("arbitrary",)),
        name="weight_to_bf16",
    )(w)


def _ada_kernel(c_ref, w_ref, b_ref, o_ref):
    c = c_ref[...]
    a = c * jax.nn.sigmoid(c)
    o_ref[0] = jnp.dot(a.astype(BF16), w_ref[0].astype(BF16), preferred_element_type=F32) + b_ref[0]


def _ada_modulation(cc, w_ada, b_ada):
    depth, d, n = w_ada.shape
    tn = 1024
    return pl.pallas_call(
        _ada_kernel,
        grid=(depth, n // tn),
        in_specs=[
            pl.BlockSpec((8, d), lambda l, j: (0, 0)),
            pl.BlockSpec((1, d, tn), lambda l, j: (l, 0, j)),
            pl.BlockSpec((1, 1, tn), lambda l, j: (l, 0, j)),
        ],
        out_specs=pl.BlockSpec((1, 8, tn), lambda l, j: (l, 0, j)),
        out_shape=jax.ShapeDtypeStruct((depth, 8, n), F32),
        compiler_params=_cparams(("arbitrary", "arbitrary")),
        name="ada_modulation",
    )(cc, w_ada, b_ada.reshape(depth, 1, n))


def _rope_kind(colblock):
    if colblock < COL_AV:
        return 0
    if COL_BQ <= colblock < COL_BV:
        return 1
    return None


def _proj_kernel(x_ref, sc_ref, sh_ref, w_ref, *rest, tn, rope):
    if rope:
        cos_ref, sin_ref, o_ref, vt_ref = rest
    else:
        (o_ref,) = rest
    xm = (x_ref[...] * (1.0 + sc_ref[0]) + sh_ref[0]).astype(BF16)
    n = w_ref.shape[1]
    for jn in range(n // tn):
        y = jnp.dot(xm, w_ref[:, jn * tn:(jn + 1) * tn], preferred_element_type=F32)
        for jb in range(tn // LANES):
            colblock = jn * (tn // LANES) + jb
            yb = y[:, jb * LANES:(jb + 1) * LANES]
            kind = _rope_kind(colblock) if rope else None
            if kind is not None:
                yb = yb * cos_ref[kind] + pltpu.roll(yb, LANES // 2, 1) * sin_ref[kind]
            if rope and colblock < COL_AK:
                yb = yb * A_Q_SCALE
            o_ref[:, colblock * LANES:(colblock + 1) * LANES] = yb.astype(o_ref.dtype)
            if rope and COL_AV <= colblock < COL_BQ:
                hd = colblock - COL_AV
                for ck in range(yb.shape[0] // KEY_CHUNK):
                    vt_ref[ck, hd * HEAD_DIM:(hd + 1) * HEAD_DIM, :] = (
                        yb[ck * KEY_CHUNK:(ck + 1) * KEY_CHUNK, :].T.astype(vt_ref.dtype))


def _mod_index(layer_slot, chunk, n_lat_tiles):
    def index(i):
        who = jnp.where(i >= n_lat_tiles, 1, 0)
        return ((layer_slot * 2 + who) * 6 + chunk, 0, 0)
    return index


def _projection(xs, modv, layer, chunk_scale, chunk_shift, w, n_lat_tiles, rope_tabs=None):
    tp, d = xs.shape
    n = w.shape[1]
    tm = TOKEN_TILE
    rope = rope_tabs is not None
    in_specs = [
        pl.BlockSpec((tm, d), lambda i: (i, 0)),
        pl.BlockSpec((1, 1, d), _mod_index(layer, chunk_scale, n_lat_tiles)),
        pl.BlockSpec((1, 1, d), _mod_index(layer, chunk_shift, n_lat_tiles)),
        pl.BlockSpec((d, n), lambda i: (0, 0), pipeline_mode=pl.Buffered(1)),
    ]
    args = [xs, modv, modv, w]
    if rope:
        in_specs += [pl.BlockSpec((2, tm, LANES), lambda i: (0, i, 0))] * 2
        args += list(rope_tabs)
    out_specs = pl.BlockSpec((tm, n), lambda i: (i, 0))
    out_shape = jax.ShapeDtypeStruct((tp, n), BF16)
    if rope:
        per = tm // KEY_CHUNK
        out_specs = [out_specs, pl.BlockSpec((per, A_HEADS * HEAD_DIM, KEY_CHUNK), lambda i: (i, 0, 0))]
        out_shape = [out_shape, jax.ShapeDtypeStruct((tp // KEY_CHUNK, A_HEADS * HEAD_DIM, KEY_CHUNK), BF16)]
    return pl.pallas_call(
        functools.partial(_proj_kernel, tn=512, rope=rope),
        grid=(tp // tm,),
        in_specs=in_specs,
        out_specs=out_specs,
        out_shape=out_shape,
        compiler_params=_cparams(("arbitrary",)),
        name="mod_projection_rope" if rope else "mod_projection",
    )(*args)


def _attn_a_kernel(lq1_ref, lk1_ref, lq2_ref, lk2_ref, subln_ref, beta_ref, q_ref, k_ref, vt_ref, o_ref,
                   qs_sc, s0_sc, s1_sc, p0_sc, p1_sc, m_sc, acc_sc, *, s_lat, tq, n_sub, n_lat_tiles, lam_init):
    i = pl.program_id(1)
    lane = lax.broadcasted_iota(jnp.int32, (1, LANES), 1)
    map1 = ((lane // 32) % 2) == 1
    q = q_ref[...]
    zero = jnp.zeros_like(q)
    qs_sc[0:tq, :] = jnp.where(map1, zero, q)
    qs_sc[tq:2 * tq, :] = jnp.where(map1, q, zero)
    m_sc[...] = jnp.full_like(m_sc, -jnp.inf)
    acc_sc[...] = jnp.zeros_like(acc_sc)
    ones_rows = jnp.where(lax.broadcasted_iota(jnp.int32, (A_ONES_ROWS, KEY_CHUNK), 0) == 0, 1.0, 0.0).astype(BF16)

    def keys(c):
        start = c * KEY_CHUNK
        if not isinstance(c, int):
            start = pl.multiple_of(start, KEY_CHUNK)
        return k_ref[pl.ds(start, KEY_CHUNK), :]

    def scores(group, s_sc):
        for j in range(n_sub):
            s_sc[j] = _dot_nt(keys(group * n_sub + j), qs_sc[...])

    ctx_chunk = s_lat // KEY_CHUNK

    def context_probs():
        s = _dot_nt(keys(ctx_chunk), qs_sc[...])
        m = jnp.max(s, axis=0, keepdims=True)
        p1_sc[0] = jnp.exp2(s - m).astype(BF16)
        for j in range(1, n_sub):
            p1_sc[j] = jnp.zeros(p1_sc.shape[1:], BF16)
        m_sc[...] = m

    def values(chunks, p_sc):
        out = None
        for j, c in enumerate(chunks):
            v1 = jnp.concatenate([vt_ref[c], ones_rows], axis=0)
            part = jnp.dot(v1, p_sc[j], preferred_element_type=F32)
            out = part if out is None else out + part
        return out

    @pl.when(i >= n_lat_tiles)
    def _():
        context_probs()
        acc_sc[...] = values([ctx_chunk], p1_sc)

    def stage(g_scores, s_next_sc, prev_chunks, p_prev_sc, s_cur_sc, p_cur_sc):
        acc_sc[...] += values(prev_chunks, p_prev_sc)
        scores(g_scores, s_next_sc)
        m_prev = m_sc[...]
        m_new = m_prev
        for j in range(n_sub):
            m_new = jnp.maximum(m_new, jnp.max(s_cur_sc[j], axis=0, keepdims=True))
        for j in range(n_sub):
            p_cur_sc[j] = jnp.exp2(s_cur_sc[j] - m_new).astype(BF16)
        acc_sc[...] *= jnp.exp2(m_prev - m_new)
        m_sc[...] = m_new

    @pl.when(i < n_lat_tiles)
    def _():
        n_groups = s_lat // (KEY_CHUNK * n_sub)
        scores(0, s0_sc)
        context_probs()

        unroll = math.gcd(n_groups, 8)
        assert unroll % 2 == 0

        def body(gu, carry):
            for u in range(unroll):
                g = unroll * gu + u
                bufs = (s1_sc, p1_sc, s0_sc, p0_sc) if u % 2 == 0 else (s0_sc, p0_sc, s1_sc, p1_sc)
                s_next, p_prev, s_cur, p_cur = bufs
                prev_chunks = [jnp.where(g == 0, ctx_chunk if j == 0 else 0, (g - 1) * n_sub + j)
                               for j in range(n_sub)]
                stage(jnp.minimum(g + 1, n_groups - 1), s_next, prev_chunks, p_prev, s_cur, p_cur)
            return carry
        lax.fori_loop(0, n_groups // unroll, body, 0)
        acc_sc[...] += values([(n_groups - 1) * n_sub + j for j in range(n_sub)], p1_sc)

    lam = (jnp.exp(jnp.sum(lq1_ref[...] * lk1_ref[...], axis=-1, keepdims=True))
           - jnp.exp(jnp.sum(lq2_ref[...] * lk2_ref[...], axis=-1, keepdims=True)) + lam_init)
    l = acc_sc[HEAD_DIM:HEAD_DIM + 1, :]
    o_t = (acc_sc[0:HEAD_DIM, 0:tq] / l[:, 0:tq] - lam * (acc_sc[0:HEAD_DIM, tq:2 * tq] / l[:, tq:2 * tq]))
    o = o_t.T
    y = o * lax.rsqrt(jnp.mean(o * o, axis=-1, keepdims=True) + LN_EPS) * subln_ref[...]
    o_ref[...] = (y * (1.0 - lam_init) * beta_ref[...]).astype(o_ref.dtype)


def _attention_a(proj, vt, lq1, lk1, lq2, lk2, subln, beta, s_lat, n_ctx, lam_init):
    tp = proj.shape[0]
    assert n_ctx == KEY_CHUNK
    tq, n_sub = 256, 2
    n_lat_tiles = s_lat // tq
    vec = lambda: pl.BlockSpec((1, A_QK_DIM), lambda h, i: (0, 0))
    return pl.pallas_call(
        functools.partial(_attn_a_kernel, s_lat=s_lat, tq=tq, n_sub=n_sub, n_lat_tiles=n_lat_tiles,
                          lam_init=lam_init),
        grid=(A_HEADS, tp // tq),
        in_specs=[
            vec(), vec(), vec(), vec(),
            pl.BlockSpec((1, HEAD_DIM), lambda h, i: (0, 0)),
            pl.BlockSpec((1, HEAD_DIM), lambda h, i: (0, h)),
            pl.BlockSpec((tq, HEAD_DIM), lambda h, i: (i, COL_AQ + h)),
            pl.BlockSpec((tp, HEAD_DIM), lambda h, i: (0, COL_AK + h)),
            pl.BlockSpec((tp // KEY_CHUNK, HEAD_DIM, KEY_CHUNK), lambda h, i: (0, h, 0)),
        ],
        out_specs=pl.BlockSpec((tq, HEAD_DIM), lambda h, i: (i, h)),
        out_shape=jax.ShapeDtypeStruct((tp, A_HEADS * HEAD_DIM), BF16),
        scratch_shapes=[
            pltpu.VMEM((2 * tq, HEAD_DIM), BF16),
            pltpu.VMEM((n_sub, KEY_CHUNK, 2 * tq), F32),
            pltpu.VMEM((n_sub, KEY_CHUNK, 2 * tq), F32),
            pltpu.VMEM((n_sub, KEY_CHUNK, 2 * tq), BF16),
            pltpu.VMEM((n_sub, KEY_CHUNK, 2 * tq), BF16),
            pltpu.VMEM((1, 2 * tq), F32),
            pltpu.VMEM((HEAD_DIM + A_ONES_ROWS, 2 * tq), F32),
        ],
        compiler_params=_cparams(("arbitrary", "arbitrary")),
        name="attn_diff",
    )(lq1, lk1, lq2, lk2, subln, beta, proj, proj, vt)


def _attn_b_kernel(sink_ref, beta_ref, q_ref, kp_ref, kc_ref, kn_ref, kx_ref, vp_ref, vc_ref, vn_ref, vx_ref,
                   o_ref, *, s_lat, tq):
    kv = pl.program_id(0)
    i = pl.program_id(1)
    scale = HEAD_DIM ** -0.5
    q0 = i * tq
    qpos = q0 + lax.broadcasted_iota(jnp.int32, (tq, 1), 0)

    def valid(kstart, n):
        kpos = kstart + lax.broadcasted_iota(jnp.int32, (1, n), 1)
        return (jnp.abs(kpos - qpos) <= WINDOW) & (kpos >= 0) & (kpos < s_lat) & (qpos < s_lat)

    ok_p = valid(q0 - WINDOW, WINDOW)
    ok_c = valid(q0, tq)
    ok_n = valid(q0 + tq, WINDOW)
    neg = -jnp.inf
    for g in range(B_GROUP):
        qg = q_ref[:, g * HEAD_DIM:(g + 1) * HEAD_DIM]
        s_x = _dot_nt(qg, kx_ref[...]) * scale
        s_p = jnp.where(ok_p, _dot_nt(qg, kp_ref[...]) * scale, neg)
        s_c = jnp.where(ok_c, _dot_nt(qg, kc_ref[...]) * scale, neg)
        s_n = jnp.where(ok_n, _dot_nt(qg, kn_ref[...]) * scale, neg)
        snk = sink_ref[kv * B_GROUP + g]
        mx = jnp.maximum(jnp.maximum(jnp.max(s_x, -1, keepdims=True), jnp.max(s_p, -1, keepdims=True)),
                         jnp.maximum(jnp.max(s_c, -1, keepdims=True), jnp.max(s_n, -1, keepdims=True)))
        mx = jnp.maximum(mx, snk)
        e_x = jnp.exp(s_x - mx)
        e_p = jnp.exp(s_p - mx)
        e_c = jnp.exp(s_c - mx)
        e_n = jnp.exp(s_n - mx)
        den = (jnp.exp(snk - mx) + jnp.sum(e_x, -1, keepdims=True) + jnp.sum(e_p, -1, keepdims=True)
               + jnp.sum(e_c, -1, keepdims=True) + jnp.sum(e_n, -1, keepdims=True))
        o = (jnp.dot(e_x.astype(BF16), vx_ref[...], preferred_element_type=F32)
             + jnp.dot(e_p.astype(BF16), vp_ref[...], preferred_element_type=F32)
             + jnp.dot(e_c.astype(BF16), vc_ref[...], preferred_element_type=F32)
             + jnp.dot(e_n.astype(BF16), vn_ref[...], preferred_element_type=F32))
        o_ref[:, g * HEAD_DIM:(g + 1) * HEAD_DIM] = (
            (o / den) * beta_ref[:, g * HEAD_DIM:(g + 1) * HEAD_DIM]).astype(o_ref.dtype)


def _attention_b(proj, sink, beta, s_lat, n_ctx):
    tp = proj.shape[0]
    tq = 256
    per = tq // WINDOW
    last_halo = tp // WINDOW - 1
    gw = B_GROUP * HEAD_DIM

    def halo_prev(col):
        return pl.BlockSpec((WINDOW, HEAD_DIM), lambda kv, i: (jnp.maximum(i * per - 1, 0), col + kv))

    def halo_next(col):
        return pl.BlockSpec((WINDOW, HEAD_DIM), lambda kv, i: (jnp.minimum((i + 1) * per, last_halo), col + kv))

    def cur(col):
        return pl.BlockSpec((tq, HEAD_DIM), lambda kv, i: (i, col + kv))

    def ctx(col):
        return pl.BlockSpec((n_ctx, HEAD_DIM), lambda kv, i: (s_lat // n_ctx, col + kv))

    return pl.pallas_call(
        functools.partial(_attn_b_kernel, s_lat=s_lat, tq=tq),
        grid=(B_KV_HEADS, tp // tq),
        in_specs=[
            pl.BlockSpec(memory_space=pltpu.SMEM),
            pl.BlockSpec((1, gw), lambda kv, i: (0, (A_HEADS * HEAD_DIM) // gw + kv)),
            pl.BlockSpec((tq, gw), lambda kv, i: (i, (COL_BQ * LANES) // gw + kv)),
            halo_prev(COL_BK), cur(COL_BK), halo_next(COL_BK), ctx(COL_BK),
            halo_prev(COL_BV), cur(COL_BV), halo_next(COL_BV), ctx(COL_BV),
        ],
        out_specs=pl.BlockSpec((tq, gw), lambda kv, i: (i, kv)),
        out_shape=jax.ShapeDtypeStruct((tp, B_HEADS * HEAD_DIM), BF16),
        compiler_params=_cparams(("arbitrary", "arbitrary")),
        name="attn_window",
    )(sink, beta, proj, proj, proj, proj, proj, proj, proj, proj, proj)


def _na_bias_tables(rpb, s_lat):
    rows = s_lat // GRID_W
    tr = TOKEN_TILE // GRID_W
    kh = min(NA_KH, rows)

    neg = -1e30
    n_heads = rpb.shape[0]
    per_c = []
    for c in range(GRID_W):
        c0 = min(max(c - NA_KW // 2, 0), GRID_W - NA_KW)
        lo = c0 - c + NA_KW - 1
        per_c.append(jnp.pad(rpb[:, :, lo:lo + NA_KW], ((0, 0), (0, 0), (c0, GRID_W - NA_KW - c0)),
                             constant_values=neg))
    colb = jnp.stack(per_c, axis=2).astype(F32)
    off = jnp.full((n_heads, GRID_W, GRID_W), neg, F32)

    hr = NA_HALO_ROWS

    def table(q_row0, n_rows):
        per_r = []
        for r in range(q_row0, q_row0 + tr):
            r0 = min(max(r - kh // 2, 0), n_rows - kh)
            assert q_row0 - hr <= r0 and r0 + kh <= q_row0 + tr + hr
            blocks = []
            for kr in range(q_row0 - hr, q_row0 + tr + hr):
                inside = r0 <= kr < r0 + kh and 0 <= kr < n_rows
                blocks.append(colb[:, kr - r + NA_KH - 1] if inside else off)
            per_r.append(jnp.concatenate(blocks, axis=-1))
        return jnp.stack(per_r, axis=1).reshape(n_heads, TOKEN_TILE, TOKEN_TILE + 2 * hr * GRID_W)

    far = 4 * tr
    out = [table(0, rows), table(far, 2 * far + tr), table(rows - tr, rows)]
    out.append(jnp.full_like(out[0], neg))
    return jnp.stack(out, axis=0)


def _attn_c_kernel(beta_ref, bias_ref, q_ref, kp_ref, kc_ref, kn_ref, kx_ref, vp_ref, vc_ref, vn_ref, vx_ref,
                   o_ref, *, tq):
    scale = HEAD_DIM ** -0.5
    q = q_ref[...]
    hq = kp_ref.shape[0]
    s_x = _dot_nt(q, kx_ref[...]) * scale
    s_p = _dot_nt(q, kp_ref[...]) * scale + bias_ref[0, 0, :, 0:hq]
    s_c = _dot_nt(q, kc_ref[...]) * scale + bias_ref[0, 0, :, hq:hq + tq]
    s_n = _dot_nt(q, kn_ref[...]) * scale + bias_ref[0, 0, :, hq + tq:2 * hq + tq]
    mx = jnp.maximum(jnp.maximum(jnp.max(s_x, -1, keepdims=True), jnp.max(s_p, -1, keepdims=True)),
                     jnp.maximum(jnp.max(s_c, -1, keepdims=True), jnp.max(s_n, -1, keepdims=True)))
    e_x = jnp.exp(s_x - mx)
    e_p = jnp.exp(s_p - mx)
    e_c = jnp.exp(s_c - mx)
    e_n = jnp.exp(s_n - mx)
    den = (jnp.sum(e_x, -1, keepdims=True) + jnp.sum(e_p, -1, keepdims=True)
           + jnp.sum(e_c, -1, keepdims=True) + jnp.sum(e_n, -1, keepdims=True))
    o = (jnp.dot(e_x.astype(BF16), vx_ref[...], preferred_element_type=F32)
         + jnp.dot(e_p.astype(BF16), vp_ref[...], preferred_element_type=F32)
         + jnp.dot(e_c.astype(BF16), vc_ref[...], preferred_element_type=F32)
         + jnp.dot(e_n.astype(BF16), vn_ref[...], preferred_element_type=F32))
    o_ref[...] = ((o / den) * beta_ref[...]).astype(o_ref.dtype)


def _attention_c(proj, bias_tabs, bias_head0, beta, s_lat, n_ctx):
    tp = proj.shape[0]
    tq = TOKEN_TILE
    hq = NA_HALO_ROWS * GRID_W
    per = tq // hq
    n_lat_tiles = s_lat // tq
    n_tiles = tp // tq
    beta_col0 = (A_HEADS + B_HEADS)

    def kind(i):
        return jnp.where(i >= n_lat_tiles, 3, jnp.where(i == 0, 0, jnp.where(i == n_lat_tiles - 1, 2, 1)))

    def prev(col):
        return pl.BlockSpec((hq, HEAD_DIM), lambda h, i: (jnp.maximum(i * per - 1, 0), col + h))

    def cur(col):
        return pl.BlockSpec((tq, HEAD_DIM), lambda h, i: (i, col + h))

    def nxt(col):
        return pl.BlockSpec((hq, HEAD_DIM), lambda h, i: (jnp.minimum((i + 1) * per, tp // hq - 1), col + h))

    def ctx(col):
        return pl.BlockSpec((n_ctx, HEAD_DIM), lambda h, i: (s_lat // n_ctx, col + h))

    return pl.pallas_call(
        functools.partial(_attn_c_kernel, tq=tq),
        grid=(C_HEADS, n_tiles),
        in_specs=[
            pl.BlockSpec((1, HEAD_DIM), lambda h, i: (0, beta_col0 + h)),
            pl.BlockSpec((1, 1, tq, tq + 2 * hq), lambda h, i: (kind(i), bias_head0 + h, 0, 0)),
            cur(COL_CQ),
            prev(COL_CK), cur(COL_CK), nxt(COL_CK), ctx(COL_CK),
            prev(COL_CV), cur(COL_CV), nxt(COL_CV), ctx(COL_CV),
        ],
        out_specs=pl.BlockSpec((tq, HEAD_DIM), lambda h, i: (i, h)),
        out_shape=jax.ShapeDtypeStruct((tp, C_HEADS * HEAD_DIM), BF16),
        compiler_params=_cparams(("arbitrary", "arbitrary")),
        name="attn_neighbourhood",
    )(beta, bias_tabs, proj, proj, proj, proj, proj, proj, proj, proj, proj)


def _outproj_kernel(ma_ref, mb_ref, mc_ref, w_ref, x_ref, gate_ref, g_ref, b_ref, o_ref, *, alpha):
    wa = A_HEADS * HEAD_DIM
    wb = wa + B_HEADS * HEAD_DIM
    y = (jnp.dot(ma_ref[...], w_ref[0:wa, :], preferred_element_type=F32)
         + jnp.dot(mb_ref[...], w_ref[wa:wb, :], preferred_element_type=F32)
         + jnp.dot(mc_ref[...], w_ref[wb:, :], preferred_element_type=F32))
    z = alpha * x_ref[...] + gate_ref[0] * y
    o_ref[...] = _layer_norm(z, g_ref[...], b_ref[...])


def _out_projection(ma, mb, mc, w_out, xs, modv, layer, n_lat_tiles, ln_g, ln_b, alpha):
    tp, d = xs.shape
    tm = TOKEN_TILE // 2
    n_lat = n_lat_tiles * (TOKEN_TILE // tm)
    row = lambda i: (i, 0)
    fixed = lambda i: (0, 0)
    return pl.pallas_call(
        functools.partial(_outproj_kernel, alpha=alpha),
        grid=(tp // tm,),
        in_specs=[
            pl.BlockSpec((tm, ma.shape[1]), row),
            pl.BlockSpec((tm, mb.shape[1]), row),
            pl.BlockSpec((tm, mc.shape[1]), row),
            pl.BlockSpec(w_out.shape, fixed, pipeline_mode=pl.Buffered(1)),
            pl.BlockSpec((tm, d), row),
            pl.BlockSpec((1, 1, d), _mod_index(layer, 2, n_lat)),
            pl.BlockSpec((1, d), fixed),
            pl.BlockSpec((1, d), fixed),
        ],
        out_specs=pl.BlockSpec((tm, d), row),
        out_shape=jax.ShapeDtypeStruct((tp, d), F32),
        compiler_params=_cparams(("arbitrary",)),
        name="out_projection_ln",
    )(ma, mb, mc, w_out, xs, modv, ln_g, ln_b)


def _take_top(vals, rounds, want_round):
    tops = []
    n, t = vals.shape
    rnd = jnp.full((n, t), float(n), F32) if want_round else None
    for r in range(rounds):
        m = jnp.max(vals, axis=0, keepdims=True)
        eq = vals == m
        if want_round:
            rnd = jnp.minimum(rnd, jnp.where(eq, float(r), float(n)))
        tops.append(m)
        vals = jnp.where(eq, -jnp.inf, vals)
    removed = jnp.sum(jnp.where(vals == -jnp.inf, 1.0, 0.0), axis=0, keepdims=True)
    return jnp.concatenate(tops, axis=0), removed, rnd


def _multiplicities(s, v):
    return jnp.concatenate([jnp.sum(jnp.where(s == v[r:r + 1], 1.0, 0.0), axis=0, keepdims=True)
                            for r in range(v.shape[0])], axis=0)


def _pair_table(a, b, op):
    half = PEER_TOPK // 2
    pieces = [op(a[0:1], b)] + [op(a[r:r + 1], b[0:half]) for r in range(1, half)] + [op(a[half:], b[0:1])]
    return jnp.concatenate(pieces, axis=0)


def _route_kernel(q_ref, keys_ref, n1_ref, rk2_ref, p1_ref, p2_ref, s_sc, v_sc, c_sc):
    k = PEER_TOPK
    tt = q_ref.shape[0]
    chunks = [slice(lc * LANES, (lc + 1) * LANES) for lc in range(tt // LANES)]
    for h in range(PEER_HEADS):
        s_sc[0] = _dot_nt(keys_ref[h, 0], q_ref[:, (2 * h) * LANES:(2 * h + 1) * LANES])
        s_sc[1] = _dot_nt(keys_ref[h, 1], q_ref[:, (2 * h + 1) * LANES:(2 * h + 2) * LANES])
        most_removed = jnp.zeros((1, LANES), F32)
        for cols in chunks:
            v1, removed1, _ = _take_top(s_sc[0, :, cols], k, False)
            v2, removed2, round2 = _take_top(s_sc[1, :, cols], k, True)
            v_sc[0, :, cols] = v1
            v_sc[1, :, cols] = v2
            rk2_ref[h, :, cols] = round2.astype(BF16)
            most_removed = jnp.maximum(most_removed, jnp.maximum(removed1, removed2))
        c_sc[...] = jnp.ones_like(c_sc)

        @pl.when(jnp.max(most_removed) > k)
        def _():
            for cols in chunks:
                s2 = s_sc[1, :, cols]
                v2 = v_sc[1, :, cols]
                c2 = _multiplicities(s2, v2)
                above = jnp.zeros_like(s2)
                for r in range(k):
                    above = above + jnp.where(v2[r:r + 1] > s2, c2[r:r + 1], 0.0)
                c_sc[0, :, cols] = _multiplicities(s_sc[0, :, cols], v_sc[0, :, cols])
                c_sc[1, :, cols] = c2
                rk2_ref[h, :, cols] = above.astype(BF16)

        for cols in chunks:
            s1 = s_sc[0, :, cols]
            s2 = s_sc[1, :, cols]
            v1 = v_sc[0, :, cols]
            v2 = v_sc[1, :, cols]
            c1 = c_sc[0, :, cols]
            c2 = c_sc[1, :, cols]
            cand = _pair_table(v1, v2, jnp.add)
            mult = _pair_table(c1, c2, jnp.multiply)
            thr = jnp.full((1, LANES), jnp.inf, F32)
            seen = jnp.zeros((1, LANES), F32)
            rem = cand
            for _ in range(k):
                m = jnp.max(rem, axis=0, keepdims=True)
                eq = rem == m
                thr = jnp.where(seen < k, m, thr)
                seen = seen + jnp.sum(jnp.where(eq, mult, 0.0), axis=0, keepdims=True)
                rem = jnp.where(eq, -jnp.inf, rem)
            pair = _pair_table(jnp.exp(v1 - v1[0:1]), jnp.exp(v2 - v2[0:1]), jnp.multiply)
            z = jnp.sum(jnp.where(cand >= thr, mult * pair, 0.0), axis=0, keepdims=True)
            n1 = jnp.zeros_like(s1)
            for r in range(k):
                n1 = n1 + jnp.where(s1 + v2[r:r + 1] >= thr, c2[r:r + 1], 0.0)
            n1_ref[h, :, cols] = n1
            p1_ref[h, :, cols] = jnp.exp(s1 - v1[0:1])
            p2_ref[h, :, cols] = (jnp.exp(s2 - v2[0:1]) / z).astype(BF16)


def _peer_route(qp, keys):
    tp = qp.shape[0]
    tt = TOKEN_TILE
    big = pl.BlockSpec((PEER_HEADS, PEER_KEYS, tt), lambda j: (0, 0, j))
    wide = jax.ShapeDtypeStruct((PEER_HEADS, PEER_KEYS, tp), F32)
    narrow = jax.ShapeDtypeStruct((PEER_HEADS, PEER_KEYS, tp), BF16)
    return pl.pallas_call(
        _route_kernel,
        grid=(tp // tt,),
        in_specs=[
            pl.BlockSpec((tt, qp.shape[1]), lambda j: (j, 0)),
            pl.BlockSpec(keys.shape, lambda j: (0, 0, 0, 0)),
        ],
        out_specs=[big, big, big, big],
        out_shape=[wide, narrow, wide, narrow],
        scratch_shapes=[
            pltpu.VMEM((2, PEER_KEYS, tt), F32),
            pltpu.VMEM((2, PEER_TOPK, tt), F32),
            pltpu.VMEM((2, PEER_TOPK, tt), F32),
        ],
        compiler_params=_cparams(("arbitrary",)),
        name="peer_route",
    )(qp, keys)


def _gelu(a):
    return 0.5 * a * (1.0 + lax.erf(a * (2.0 ** -0.5)))


def _peer_expert_kernel(x_ref, sc_ref, sh_ref, gate_ref, g_ref, b_ref, u_ref, v_ref,
                        n1_ref, rk2_ref, p1_ref, p2_ref, o_ref, ux_sc, w0_sc, w1_sc, at0_sc, at1_sc, acc_sc,
                        *, alpha, te, tt, n_e):
    i = pl.program_id(1)
    pack = BF16_SUBLANES
    d = acc_sc.shape[1]
    n_blk = te // PEER_KEYS

    @pl.when(i == 0)
    def _():
        ux_sc[...] = (x_ref[...] * (1.0 + sc_ref[0]) + sh_ref[0]).astype(BF16)
        acc_sc[...] = jnp.zeros_like(acc_sc)
        w1_sc[...] = jnp.zeros_like(w1_sc)

    zero = jnp.zeros((pack, LANES), BF16)

    at_bufs = (at0_sc, at1_sc)
    eg = at0_sc.shape[0] // PEER_KEYS
    tg = at0_sc.shape[1] // LANES
    n_eg, n_tg = n_blk // eg, tt // LANES // tg
    n_pieces = n_eg * n_tg
    dn = d // n_pieces

    def hidden(r):
        ge, gt = divmod(r, n_tg)
        at_bufs[r % 2][...] = _dot_nt(u_ref[ge * eg * PEER_KEYS:(ge + 1) * eg * PEER_KEYS, :],
                                      ux_sc[gt * tg * LANES:(gt + 1) * tg * LANES, :])

    def cell(cur_sc, r, ca, cl):
        ge, gt = divmod(r, n_tg)
        a, lc = ge * eg + ca, gt * tg + cl
        cols = slice(lc * LANES, (lc + 1) * LANES)
        a_t = at_bufs[r % 2].at[ca * PEER_KEYS:(ca + 1) * PEER_KEYS, cl * LANES:(cl + 1) * LANES]
        n1 = [jnp.broadcast_to(n1_ref[h, a:a + 1, cols], (pack, LANES)).astype(BF16) for h in range(PEER_HEADS)]
        p1 = [jnp.broadcast_to(p1_ref[h, a:a + 1, cols], (pack, LANES)).astype(BF16) for h in range(PEER_HEADS)]
        pieces = []
        for rg in range(PEER_KEYS // pack):
            rows = slice(rg * pack, (rg + 1) * pack)
            g = zero
            for h in range(PEER_HEADS):
                sel_p1 = jnp.minimum(jnp.maximum(n1[h] - rk2_ref[h, rows, cols], zero), p1[h])
                g = g + sel_p1 * p2_ref[h, rows, cols]
            pieces.append(g * _gelu(a_t[rows, :]).astype(BF16))
        blk = jnp.concatenate(pieces, axis=0)
        cur_sc[cols, a * PEER_KEYS:(a + 1) * PEER_KEYS] = blk.T

    def step(cur_sc, prev_sc):
        hidden(0)
        for r in range(n_pieces):
            cells = [(ca, cl) for ca in range(eg) for cl in range(tg)]
            half = len(cells) // 2
            if r + 1 < n_pieces:
                hidden(r + 1)
            for ca, cl in cells[:half]:
                cell(cur_sc, r, ca, cl)
            acc_sc[:, r * dn:(r + 1) * dn] += jnp.dot(prev_sc[...], v_ref[:, r * dn:(r + 1) * dn],
                                                      preferred_element_type=F32)
            for ca, cl in cells[half:]:
                cell(cur_sc, r, ca, cl)

    @pl.when((i < n_e) & (i % 2 == 0))
    def _():
        step(w0_sc, w1_sc)

    @pl.when((i < n_e) & (i % 2 == 1))
    def _():
        step(w1_sc, w0_sc)

    @pl.when(i == n_e)
    def _():
        last_sc = w1_sc if n_e % 2 == 0 else w0_sc
        y = acc_sc[...] + jnp.dot(last_sc[...], v_ref[...], preferred_element_type=F32)
        z = alpha * x_ref[...] + gate_ref[0] * y
        o_ref[...] = _layer_norm(z, g_ref[...], b_ref[...])


def _peer_experts(xs, modv, layer, n_lat_tiles, n_out_tiles, ln_g, ln_b, u, v, route, alpha):
    tp, d = xs.shape
    tt = TOKEN_TILE
    te = 8 * PEER_KEYS
    n_exp = u.shape[0]
    n1, rk2, p1, p2 = route
    once = pl.Buffered(1)
    tok = lambda j, i: (j, 0)
    fixed = lambda j, i: (0, 0)
    mod = lambda chunk: (lambda j, i: _mod_index(layer, chunk, n_lat_tiles)(j))
    rt = lambda: pl.BlockSpec((PEER_HEADS, PEER_KEYS, tt), lambda j, i: (0, 0, j), pipeline_mode=once)
    n_e = n_exp // te
    build = lambda i: jnp.minimum(i, n_e - 1)
    drain = lambda i: jnp.maximum(i - 1, 0)
    rt1 = lambda: pl.BlockSpec((PEER_HEADS, te // PEER_KEYS, tt), lambda j, i: (0, build(i), j))
    return pl.pallas_call(
        functools.partial(_peer_expert_kernel, alpha=alpha, te=te, tt=tt, n_e=n_e),
        grid=(n_out_tiles, n_e + 1),
        in_specs=[
            pl.BlockSpec((tt, d), tok, pipeline_mode=once),
            pl.BlockSpec((1, 1, d), mod(4)),
            pl.BlockSpec((1, 1, d), mod(3)),
            pl.BlockSpec((1, 1, d), mod(5)),
            pl.BlockSpec((1, d), fixed),
            pl.BlockSpec((1, d), fixed),
            pl.BlockSpec((te, d), lambda j, i: (build(i), 0)),
            pl.BlockSpec((te, d), lambda j, i: (drain(i), 0)),
            rt1(), rt(), rt1(), rt(),
        ],
        out_specs=pl.BlockSpec((tt, d), tok),
        out_shape=jax.ShapeDtypeStruct((n_out_tiles * tt, d), F32),
        scratch_shapes=[
            pltpu.VMEM((tt, d), BF16),
            pltpu.VMEM((tt, te), BF16),
            pltpu.VMEM((tt, te), BF16),
            pltpu.VMEM((2 * PEER_KEYS, 2 * LANES), F32),
            pltpu.VMEM((2 * PEER_KEYS, 2 * LANES), F32),
            pltpu.VMEM((tt, d), F32),
        ],
        compiler_params=_cparams(("arbitrary", "arbitrary")),
        name="peer_experts_ln",
    )(xs, modv, modv, modv, ln_g, ln_b, u, v, n1, rk2, p1, p2)


def _permute_qk_columns(w):
    d = w.shape[0]
    a_end, b_start, b_end = COL_AV * LANES, COL_BQ * LANES, COL_BV * LANES
    wa = w[:, :a_end].reshape(d, COL_AV, 2, A_QK_DIM // 2, 2)
    wa = wa.transpose(0, 1, 4, 2, 3).reshape(d, a_end)
    wb = w[:, b_start:b_end].reshape(d, COL_BV - COL_BQ, HEAD_DIM // 2, 2)
    wb = wb.transpose(0, 1, 3, 2).reshape(d, b_end - b_start)
    return jnp.concatenate([wa, w[:, a_end:b_start], wb, w[:, b_end:]], axis=1)


def _rope_tables(s_lat, tp):
    t = jnp.arange(s_lat)
    row = (t // GRID_W).astype(F32)
    col = (t % GRID_W).astype(F32)
    lane = np.arange(LANES)
    sign = jnp.asarray(np.where(lane < LANES // 2, -1.0, 1.0), F32)

    def tabs(dim, lane_to_pair):
        d_axis = dim // 2
        inv = ROPE_BASE ** (-jnp.arange(0, d_axis, 2, dtype=F32) / d_axis)
        ang = jnp.concatenate([row[:, None] * inv, col[:, None] * inv], axis=-1)
        ang = ang[:, lane_to_pair]
        pad = ((0, tp - s_lat), (0, 0))
        return (jnp.pad(jnp.cos(ang), pad, constant_values=1.0),
                jnp.pad(jnp.sin(ang) * sign, pad, constant_values=0.0))

    ca, sa = tabs(A_QK_DIM, lane % 32)
    cb, sb = tabs(HEAD_DIM, lane % 64)
    return jnp.stack([ca, cb]), jnp.stack([sa, sb])


def kernel(x, c, ctx, c_ctx, w_ada, b_ada, w_in, w_out, beta_out, ln1_g, ln1_b, ln2_g, ln2_b, diff_lq1, diff_lk1, diff_lq2, diff_lk2, diff_subln, sink, na_rpb, peer_wq, peer_keys, peer_u, peer_v):
    depth = w_ada.shape[0]
    _, s_lat, d = x.shape
    n_ctx = ctx.shape[1]
    assert x.shape[0] == 1 and s_lat % TOKEN_TILE == 0 and n_ctx == 256 and s_lat % n_ctx == 0
    n_lat_tiles = s_lat // TOKEN_TILE
    tp = (n_lat_tiles + 1) * TOKEN_TILE
    alpha = (2 * depth) ** 0.25

    xs = jnp.concatenate([x[0], ctx[0], jnp.zeros((tp - s_lat - n_ctx, d), x.dtype)], axis=0)
    cc = jnp.concatenate([c[0:1], c_ctx[None, :], jnp.zeros((6, d), c.dtype)], axis=0)
    mod = _ada_modulation(cc, w_ada, b_ada)
    modv = mod[:, :2, :].reshape(depth * 2 * 6, 1, d)
    rope_tabs = _rope_tables(s_lat, tp)
    na_tabs = _na_bias_tables(na_rpb.reshape((depth * C_HEADS,) + na_rpb.shape[2:]), s_lat)

    for l in range(depth):
        lam_init = 0.8 - 0.6 * math.exp(-0.3 * l)
        last = l == depth - 1
        w_in_l = _permute_qk_columns(_to_bf16(w_in, l))
        proj, vt = _projection(xs, modv, l, 1, 0, w_in_l, n_lat_tiles, rope_tabs)
        beta = beta_out[l][None, :]
        ma = _attention_a(proj, vt, diff_lq1[l][None], diff_lk1[l][None], diff_lq2[l][None], diff_lk2[l][None],
                          diff_subln[l][None], beta, s_lat, n_ctx, lam_init)
        mb = _attention_b(proj, sink[l], beta, s_lat, n_ctx)
        mc = _attention_c(proj, na_tabs, l * C_HEADS, beta, s_lat, n_ctx)
        xs = _out_projection(ma, mb, mc, _to_bf16(w_out, l), xs, modv, l, n_lat_tiles,
                             ln1_g[l][None], ln1_b[l][None], alpha)
        qp = _projection(xs, modv, l, 4, 3, _to_bf16(peer_wq, l), n_lat_tiles)
        route = _peer_route(qp, peer_keys[l].astype(BF16))
        xs = _peer_experts(xs, modv, l, n_lat_tiles, n_lat_tiles if last else n_lat_tiles + 1,
                           ln2_g[l][None], ln2_b[l][None], _to_bf16(peer_u, l), _to_bf16(peer_v, l),
                           route, alpha)
    return xs[None]
```
